```python
import jax, jax.numpy as jnp
from jax import lax
import numpy as np

D_MODEL = 1024
BATCH = 8
SEQ = 4096
DEPTH = 1

MEM_LEN = 256
EPS = 1e-6

POOL_WINDOWS = (2, 4, 8, 16)
POOL_GROUPS = len(POOL_WINDOWS)
POOL_GROUP_DIM = D_MODEL // 8
POOL_WIDTH = POOL_GROUPS * POOL_GROUP_DIM
MAX_WINDOW = max(POOL_WINDOWS)

SWA_HEAD_DIM = 64
SWA_HEADS = D_MODEL // SWA_HEAD_DIM
SWA_KV_HEADS = 2
SWA_GROUP = SWA_HEADS // SWA_KV_HEADS
SWA_WIDTH = SWA_HEADS * SWA_HEAD_DIM
SWA_KV_WIDTH = SWA_KV_HEADS * SWA_HEAD_DIM
WINDOW = 128
BLOCK = 128
ROPE_THETA = 10000.0

MEM_HEADS = 4
MEM_HEAD_DIM = D_MODEL // 8
MEM_WIDTH = MEM_HEADS * MEM_HEAD_DIM

N_BRANCHES = 3
IN_WIDTH = POOL_WIDTH + SWA_WIDTH + 2 * SWA_KV_WIDTH + MEM_WIDTH + N_BRANCHES * D_MODEL

PEER_HEADS = 8
N_KEYS = 128
N_EXPERTS = N_KEYS * N_KEYS
PEER_KEY_DIM = 256
PEER_HALF = PEER_KEY_DIM // 2
PEER_TOPK = 16
PEER_CHUNK = 128

NEG_INF = -1e30

kernel_name = "hybrid_pool_swa_mem_peer_block"


def rmsnorm(x, g):
    xf = x.astype(jnp.float32)
    y = xf * lax.rsqrt(jnp.mean(xf * xf, axis=-1, keepdims=True) + EPS)
    return (y * g.astype(jnp.float32)).astype(x.dtype)


def pool_mixer(u, w_pool, pool_scale):
    B, S, _ = u.shape
    uf = u.astype(jnp.float32)
    cs = jnp.pad(jnp.cumsum(uf, axis=1), ((0, 0), (MAX_WINDOW, 0), (0, 0)))
    t = jnp.arange(S)
    outs = []
    for gi, w in enumerate(POOL_WINDOWS):
        lo_c, hi_c = gi * POOL_GROUP_DIM, (gi + 1) * POOL_GROUP_DIM
        upper = cs[:, MAX_WINDOW:, lo_c:hi_c]
        lower = cs[:, MAX_WINDOW - w:MAX_WINDOW - w + S, lo_c:hi_c]
        cnt = jnp.minimum(t + 1, w).astype(jnp.float32)[None, :, None]
        outs.append((upper - lower) / cnt - uf[..., lo_c:hi_c])
    pooled = jnp.stack(outs, axis=2)
    mixed = jnp.einsum('bsgc,gcd->bsgd', pooled, w_pool.astype(jnp.float32))
    return (mixed.reshape(B, S, POOL_WIDTH) * pool_scale.astype(jnp.float32)).astype(u.dtype)


def rope(x, pos):
    half = x.shape[-1] // 2
    inv = ROPE_THETA ** (-jnp.arange(half, dtype=jnp.float32) / half)
    ang = pos.astype(jnp.float32)[..., None] * inv
    cos = jnp.cos(ang)[:, :, None, :]
    sin = jnp.sin(ang)[:, :, None, :]
    xf = x.astype(jnp.float32)
    x1, x2 = xf[..., :half], xf[..., half:]
    return jnp.concatenate([x1 * cos - x2 * sin, x2 * cos + x1 * sin], axis=-1).astype(x.dtype)


def swa_with_sinks(q, k, v, sinks):
    B, S = q.shape[0], q.shape[1]
    nb = S // BLOCK
    qb = q.reshape(B, nb, BLOCK, SWA_KV_HEADS, SWA_GROUP, SWA_HEAD_DIM)

    def band(a):
        ab = a.reshape(B, nb, BLOCK, SWA_KV_HEADS, SWA_HEAD_DIM)
        prev = jnp.pad(ab, ((0, 0), (1, 0), (0, 0), (0, 0), (0, 0)))[:, :-1]
        return jnp.concatenate([prev, ab], axis=2)

    kband, vband = band(k), band(v)
    s = jnp.einsum('bnqhgd,bnkhd->bnhgqk', qb.astype(jnp.float32), kband.astype(jnp.float32))
    s = s * (SWA_HEAD_DIM ** -0.5)
    qi = jnp.arange(BLOCK)[:, None] + BLOCK
    ki = jnp.arange(2 * BLOCK)[None, :]
    rel = qi - ki
    mask = (rel >= 0) & (rel < WINDOW)
    valid_prev = (jnp.arange(nb)[:, None, None] > 0) | (ki[None] >= BLOCK)
    mask = mask[None] & valid_prev
    s = jnp.where(mask[None, :, None, None], s, NEG_INF)
    sink = sinks.astype(jnp.float32).reshape(SWA_KV_HEADS, SWA_GROUP)[None, None, :, :, None, None]
    m = jnp.maximum(jnp.max(s, axis=-1, keepdims=True), sink)
    p = jnp.exp(s - m)
    p = p / (jnp.sum(p, axis=-1, keepdims=True) + jnp.exp(sink - m))
    o = jnp.einsum('bnhgqk,bnkhd->bnqhgd', p.astype(v.dtype), vband)
    return o.reshape(B, S, SWA_WIDTH)


def mem_attention(qm, mem_n, w_mem_kv):
    B, S, _ = qm.shape
    kv = mem_n @ w_mem_kv
    km, vm = jnp.split(kv, 2, axis=-1)
    M = mem_n.shape[1]
    q = qm.reshape(B, S, MEM_HEADS, MEM_HEAD_DIM)
    km = km.reshape(B, M, MEM_HEADS, MEM_HEAD_DIM)
    vm = vm.reshape(B, M, MEM_HEADS, MEM_HEAD_DIM)
    s = jnp.einsum('bshd,bmhd->bhsm', q.astype(jnp.float32), km.astype(jnp.float32))
    p = jax.nn.softmax(s * (MEM_HEAD_DIM ** -0.5), axis=-1)
    o = jnp.einsum('bhsm,bmhd->bshd', p.astype(vm.dtype), vm)
    return o.reshape(B, S, MEM_WIDTH)


def peer(h, w_q, sub_keys, expert_u, expert_v):
    B, S, D = h.shape
    T = B * S
    hf = h.reshape(T, D)
    q = (hf @ w_q).reshape(T, PEER_HEADS, 2, PEER_HALF)
    sc = jnp.einsum('thpc,hpnc->thpn', q.astype(jnp.float32), sub_keys.astype(jnp.float32))
    s_top, i_top = lax.top_k(sc, PEER_TOPK)
    cand = (s_top[:, :, 0, :, None] + s_top[:, :, 1, None, :]).reshape(T, PEER_HEADS, PEER_TOPK * PEER_TOPK)
    ids_cand = (i_top[:, :, 0, :, None] * N_KEYS + i_top[:, :, 1, None, :]).reshape(T, PEER_HEADS, PEER_TOPK * PEER_TOPK)
    s_fin, pos = lax.top_k(cand, PEER_TOPK)
    ids = jnp.take_along_axis(ids_cand, pos, axis=-1)
    gates = jax.nn.softmax(s_fin, axis=-1).astype(h.dtype)
    nc = T // PEER_CHUNK
    HK = PEER_HEADS * PEER_TOPK

    def expert_chunk(args):
        xc, idc, gc = args
        u = expert_u[idc]
        a = jax.nn.gelu(jnp.einsum('cd,ckd->ck', xc, u))
        vv = expert_v[idc]
        return jnp.einsum('ck,ckd->cd', gc * a, vv)

    y = lax.map(expert_chunk, (hf.reshape(nc, PEER_CHUNK, D),
                               ids.reshape(nc, PEER_CHUNK, HK),
                               gates.reshape(nc, PEER_CHUNK, HK)))
    return y.reshape(B, S, D)


def setup_inputs(seed: int = 0) -> dict:
    key = jax.random.key(seed)
    ks = jax.random.split(key, 24)
    L, D = DEPTH, D_MODEL
    nrm = jax.random.normal
    x = nrm(ks[0], (BATCH, SEQ, D), jnp.float32)
    mem = nrm(ks[1], (BATCH, MEM_LEN, D), jnp.float32)
    positions = (jnp.arange(SEQ, dtype=jnp.int32)[None, :]
                 + jax.random.randint(ks[2], (BATCH, 1), 0, 1024, dtype=jnp.int32))
    return {
        "x": x,
        "mem": mem,
        "positions": positions,
        "norm1_g": 1.0 + 0.05 * nrm(ks[3], (L, D), jnp.float32),
        "mem_norm_g": 1.0 + 0.05 * nrm(ks[4], (L, D), jnp.float32),
        "w_in": nrm(ks[5], (L, D, IN_WIDTH), jnp.float32) * D ** -0.5,
        "w_pool": nrm(ks[6], (L, POOL_GROUPS, POOL_GROUP_DIM, POOL_GROUP_DIM), jnp.float32) * POOL_GROUP_DIM ** -0.5,
        "pool_scale": 1.0 + 0.1 * nrm(ks[7], (L, POOL_WIDTH), jnp.float32),
        "attn_sinks": 0.5 * nrm(ks[8], (L, SWA_HEADS), jnp.float32),
        "w_mem_kv": nrm(ks[9], (L, D, 2 * MEM_WIDTH), jnp.float32) * D ** -0.5,
        "w_branch_pool": nrm(ks[10], (L, POOL_WIDTH, D), jnp.float32) * POOL_WIDTH ** -0.5,
        "w_branch_swa": nrm(ks[11], (L, SWA_WIDTH, D), jnp.float32) * SWA_WIDTH ** -0.5,
        "w_branch_mem": nrm(ks[12], (L, MEM_WIDTH, D), jnp.float32) * MEM_WIDTH ** -0.5,
        "w_out": nrm(ks[13], (L, D, D), jnp.float32) * D ** -0.5,
        "norm2_g": 1.0 + 0.05 * nrm(ks[14], (L, D), jnp.float32),
        "peer_w_q": nrm(ks[15], (L, D, PEER_HEADS * PEER_KEY_DIM), jnp.float32) * D ** -0.5,
        "peer_sub_keys": nrm(ks[16], (L, PEER_HEADS, 2, N_KEYS, PEER_HALF), jnp.float32) * PEER_HALF ** -0.5,
        "peer_u": nrm(ks[17], (L, N_EXPERTS, D), jnp.float32) * D ** -0.5,
        "peer_v": nrm(ks[18], (L, N_EXPERTS, D), jnp.float32) * 0.5,
        "final_norm_g": 1.0 + 0.05 * nrm(ks[19], (D,), jnp.float32),
    }


def reference(x, mem, positions, norm1_g, mem_norm_g, w_in, w_pool, pool_scale, attn_sinks,
              w_mem_kv, w_branch_pool, w_branch_swa, w_branch_mem, w_out, norm2_g,
              peer_w_q, peer_sub_keys, peer_u, peer_v, final_norm_g):
    B, S, D = x.shape
    splits = np.cumsum([POOL_WIDTH, SWA_WIDTH, SWA_KV_WIDTH, SWA_KV_WIDTH, MEM_WIDTH]).tolist()
    for l in range(DEPTH):
        h = rmsnorm(x, norm1_g[l])
        z = h @ w_in[l]
        u_pool, q, k, v, qm, gate_logits = jnp.split(z, splits, axis=-1)
        y_pool = pool_mixer(u_pool, w_pool[l], pool_scale[l])
        q = rope(q.reshape(B, S, SWA_HEADS, SWA_HEAD_DIM), positions)
        k = rope(k.reshape(B, S, SWA_KV_HEADS, SWA_HEAD_DIM), positions)
        v = v.reshape(B, S, SWA_KV_HEADS, SWA_HEAD_DIM)
        y_swa = swa_with_sinks(q, k, v, attn_sinks[l])
        y_mem = mem_attention(qm, rmsnorm(mem, mem_norm_g[l]), w_mem_kv[l])
        g = jax.nn.sigmoid(gate_logits.reshape(B, S, N_BRANCHES, D))
        merged = (g[:, :, 0] * (y_pool @ w_branch_pool[l])
                  + g[:, :, 1] * (y_swa @ w_branch_swa[l])
                  + g[:, :, 2] * (y_mem @ w_branch_mem[l]))
        x = x + merged @ w_out[l]
        h2 = rmsnorm(x, norm2_g[l])
        x = x + peer(h2, peer_w_q[l], peer_sub_keys[l], peer_u[l], peer_v[l])
    return rmsnorm(x, final_norm_g)
```

```python
import functools

import jax
import jax.numpy as jnp
from jax import lax
from jax.experimental import pallas as pl
from jax.experimental.pallas import tpu as pltpu
from jax.experimental.pallas import tpu_sc as plsc

F32 = jnp.float32
BF16 = jnp.bfloat16
I32 = jnp.int32
U32 = jnp.uint32

D_MODEL = 1024
EPS = 1e-6
NEG_INF = -1e30

POOL_WINDOWS = (2, 4, 8, 16)
POOL_GROUP_DIM = 128
POOL_WIDTH = 512
MAX_WINDOW = 16

SWA_HEAD_DIM = 64
SWA_HEADS = 16
SWA_WIDTH = 1024
SWA_KV_WIDTH = 128
SWA_BLOCK = 128
ROPE_THETA = 10000.0

MEM_HEADS = 4
MEM_HEAD_DIM = 128
MEM_WIDTH = 512
GATE_WIDTH = 3 * D_MODEL
IN_WIDTH = POOL_WIDTH + SWA_WIDTH + 2 * SWA_KV_WIDTH + MEM_WIDTH + GATE_WIDTH

PEER_HEADS = 8
N_KEYS = 128
PEER_HALF = 128
PEER_TOPK = 16
PEER_PICKS = PEER_HEADS * PEER_TOPK
PEER_Q_WIDTH = PEER_HEADS * 2 * PEER_HALF
PACKED_WIDTH = D_MODEL // 2

SC_CORES = 2
SC_SUBCORES = 16
SC_WORKERS = SC_CORES * SC_SUBCORES
SC_GATHER_WINDOW = 128

VMEM_LIMIT = 48 * 1024 * 1024

TILE_INPROJ = 256
TILE_MERGE = 256
TILE_TOPK = 256
TILE_EXPERT = 16
EXPERT_CHUNKS = 8


def _const_spec(shape):
    nd = len(shape)
    return pl.BlockSpec(shape, lambda *_: (0,) * nd, pipeline_mode=pl.Buffered(1))


def _rms(x, g):
    ms = jnp.mean(x * x, axis=-1, keepdims=True)
    return x * lax.rsqrt(ms + EPS) * g


def _dot_nt(a, b):
    return lax.dot_general(a, b, (((1,), (1,)), ((), ())), preferred_element_type=F32)


def _inproj_body(x_ref, g_ref, w_ref, up_ref, q_ref, k_ref, v_ref, qm_ref, gl_ref):
    h = _rms(x_ref[...], g_ref[...]).astype(BF16)

    def seg(a, b):
        return jnp.dot(h, w_ref[:, a:b], preferred_element_type=F32)

    o = 0
    up_ref[...] = seg(o, o + POOL_WIDTH)
    o += POOL_WIDTH
    q_ref[...] = seg(o, o + SWA_WIDTH)
    o += SWA_WIDTH
    k_ref[...] = seg(o, o + SWA_KV_WIDTH)
    o += SWA_KV_WIDTH
    v_ref[...] = seg(o, o + SWA_KV_WIDTH).astype(BF16)
    o += SWA_KV_WIDTH
    qm_ref[...] = seg(o, o + MEM_WIDTH).astype(BF16)
    o += MEM_WIDTH
    gl_ref[...] = seg(o, o + GATE_WIDTH)


def _inproj(x2, g, w_bf):
    t = x2.shape[0]
    tm = TILE_INPROJ
    row = lambda w: pl.BlockSpec((tm, w), lambda i: (i, 0))
    return pl.pallas_call(
        _inproj_body,
        grid=(t // tm,),
        in_specs=[row(D_MODEL), _const_spec((1, D_MODEL)), _const_spec((D_MODEL, IN_WIDTH))],
        out_specs=[row(POOL_WIDTH), row(SWA_WIDTH), row(SWA_KV_WIDTH), row(SWA_KV_WIDTH),
                   row(MEM_WIDTH), row(GATE_WIDTH)],
        out_shape=[
            jax.ShapeDtypeStruct((t, POOL_WIDTH), F32),
            jax.ShapeDtypeStruct((t, SWA_WIDTH), F32),
            jax.ShapeDtypeStruct((t, SWA_KV_WIDTH), F32),
            jax.ShapeDtypeStruct((t, SWA_KV_WIDTH), BF16),
            jax.ShapeDtypeStruct((t, MEM_WIDTH), BF16),
            jax.ShapeDtypeStruct((t, GATE_WIDTH), F32),
        ],
        compiler_params=pltpu.CompilerParams(
            dimension_semantics=("parallel",), vmem_limit_bytes=VMEM_LIMIT),
        name="inproj",
    )(x2, g, w_bf)


def _swa_body(sink_ref, q_ref, kc_ref, kp_ref, vc_ref, vp_ref, pc_ref, pp_ref, invf_ref, o_ref):
    n = pl.program_id(1)
    invf = invf_ref[...]
    lane = lax.broadcasted_iota(I32, (1, 2 * SWA_HEAD_DIM), 1)
    first_half = (lane % SWA_HEAD_DIM) < (SWA_HEAD_DIM // 2)
    lo_head = lane < SWA_HEAD_DIM

    def cos_sin(p_ref):
        ang = p_ref[...].astype(F32) * invf
        return jnp.cos(ang), jnp.sin(ang)

    def rope(x, cs):
        partner = jnp.where(first_half, -pltpu.roll(x, 96, 1), pltpu.roll(x, 32, 1))
        return x * cs[0] + partner * cs[1]

    cs_c = cos_sin(pc_ref)
    cs_p = cos_sin(pp_ref)
    k_all = jnp.concatenate([rope(kp_ref[...], cs_p), rope(kc_ref[...], cs_c)], axis=0)
    v_all = jnp.concatenate([vp_ref[...].astype(F32), vc_ref[...].astype(F32)], axis=0)

    def split(a):
        sw = pltpu.roll(a, SWA_HEAD_DIM, 1)
        zero = jnp.zeros_like(a)
        left = (jnp.where(lo_head, a, zero).astype(BF16), jnp.where(lo_head, sw, zero).astype(BF16))
        right = (jnp.where(lo_head, zero, sw).astype(BF16), jnp.where(lo_head, zero, a).astype(BF16))
        return left, right

    k_left, k_right = split(k_all)
    v_left, v_right = split(v_all)

    qi = lax.broadcasted_iota(I32, (SWA_BLOCK, 2 * SWA_BLOCK), 0)
    ki = lax.broadcasted_iota(I32, (SWA_BLOCK, 2 * SWA_BLOCK), 1)
    allowed = (ki > qi) & (ki <= qi + SWA_BLOCK) & ((ki >= SWA_BLOCK) | (n > 0))
    scale = SWA_HEAD_DIM ** -0.5

    for c in range(SWA_HEADS // 2):
        g = c // (SWA_HEADS // 4)
        qc = rope(q_ref[:, c * 128:(c + 1) * 128], cs_c).astype(BF16)
        acc = jnp.zeros((SWA_BLOCK, 128), F32)
        for par, (kx, vx) in enumerate(((k_left[g], v_left[g]), (k_right[g], v_right[g]))):
            sink = sink_ref[2 * c + par]
            s = _dot_nt(qc, kx) * scale
            s = jnp.where(allowed, s, NEG_INF)
            m = jnp.maximum(jnp.max(s, axis=-1, keepdims=True), sink)
            p = jnp.exp(s - m)
            den = jnp.sum(p, axis=-1, keepdims=True) + jnp.exp(sink - m)
            pn = (p / den).astype(BF16)
            acc = acc + jnp.dot(pn, vx, preferred_element_type=F32)
        o_ref[:, c * 128:(c + 1) * 128] = acc.astype(BF16)


def _swa(q, k, v, pos_col, sinks, invf, batch, seq):
    nb = seq // SWA_BLOCK
    t = batch * seq
    cur = lambda w: pl.BlockSpec((SWA_BLOCK, w), lambda b, n: (b * nb + n, 0))
    prev = lambda w: pl.BlockSpec((SWA_BLOCK, w), lambda b, n: (b * nb + jnp.maximum(n - 1, 0), 0))
    return pl.pallas_call(
        _swa_body,
        grid=(batch, nb),
        in_specs=[
            pl.BlockSpec(memory_space=pltpu.SMEM),
            cur(SWA_WIDTH), cur(SWA_KV_WIDTH), prev(SWA_KV_WIDTH),
            cur(SWA_KV_WIDTH), prev(SWA_KV_WIDTH), cur(1), prev(1),
            pl.BlockSpec((1, 128), lambda b, n: (0, 0)),
        ],
        out_specs=cur(SWA_WIDTH),
        out_shape=jax.ShapeDtypeStruct((t, SWA_WIDTH), BF16),
        compiler_params=pltpu.CompilerParams(
            dimension_semantics=("parallel", "parallel"), vmem_limit_bytes=VMEM_LIMIT),
        name="swa",
    )(sinks, q, k, k, v, v, pos_col, pos_col, invf)


def _memkv_body(mem_ref, g_ref, w_ref, km_ref, vm_ref):
    mn = _rms(mem_ref[0], g_ref[...]).astype(BF16)
    kv = jnp.dot(mn, w_ref[...], preferred_element_type=F32)
    km_ref[0] = kv[:, :MEM_WIDTH].astype(BF16)
    vm_ref[0] = kv[:, MEM_WIDTH:].astype(BF16)


def _memkv(mem, g, w_bf):
    b, m, _ = mem.shape
    blk = lambda w: pl.BlockSpec((1, m, w), lambda i: (i, 0, 0))
    return pl.pallas_call(
        _memkv_body,
        grid=(b,),
        in_specs=[blk(D_MODEL), _const_spec((1, D_MODEL)), _const_spec((D_MODEL, 2 * MEM_WIDTH))],
        out_specs=[blk(MEM_WIDTH), blk(MEM_WIDTH)],
        out_shape=[jax.ShapeDtypeStruct((b, m, MEM_WIDTH), BF16)] * 2,
        compiler_params=pltpu.CompilerParams(
            dimension_semantics=("parallel",), vmem_limit_bytes=VMEM_LIMIT),
        name="memkv",
    )(mem, g, w_bf)


def _merge_body(tiles_per_seq, x_ref, up_ref, halo_ref, ys_ref, qm_ref, gl_ref, km_ref, vm_ref,
                wpool_ref, pscale_ref, wbp_ref, wbs_ref, wbm_ref, wout_ref, g2_ref, wq_ref,
                x1_ref, h2_ref, qp_ref):
    tm = x_ref.shape[0]
    tile_in_seq = pl.program_id(0) % tiles_per_seq

    halo = jnp.where(tile_in_seq > 0, halo_ref[...], 0.0)
    t_in_seq = tile_in_seq * tm + lax.broadcasted_iota(I32, (tm, 1), 0)
    pooled = []
    for gi, w in enumerate(POOL_WINDOWS):
        cols = slice(gi * POOL_GROUP_DIM, (gi + 1) * POOL_GROUP_DIM)
        u = up_ref[:, cols]
        s = jnp.concatenate([halo[:, cols], u], axis=0)
        sh = 1
        while sh < w:
            s = s + pltpu.roll(s, sh, 0)
            sh *= 2
        cnt = jnp.minimum(t_in_seq + 1, w).astype(F32)
        pg = s[MAX_WINDOW:, :] / cnt - u
        mixed = jnp.dot(pg.astype(BF16), wpool_ref[gi], preferred_element_type=F32)
        pooled.append(mixed)
    y_pool = (jnp.concatenate(pooled, axis=1) * pscale_ref[...]).astype(BF16)

    mem_scale = MEM_HEAD_DIM ** -0.5
    y_mem = []
    for hh in range(MEM_HEADS):
        cols = slice(hh * MEM_HEAD_DIM, (hh + 1) * MEM_HEAD_DIM)
        s = _dot_nt(qm_ref[:, cols], km_ref[0, :, cols]) * mem_scale
        e = jnp.exp(s - jnp.max(s, axis=-1, keepdims=True))
        p = (e / jnp.sum(e, axis=-1, keepdims=True)).astype(BF16)
        y_mem.append(jnp.dot(p, vm_ref[0, :, cols], preferred_element_type=F32))
    y_mem = jnp.concatenate(y_mem, axis=1).astype(BF16)

    def gate(j):
        return jax.nn.sigmoid(gl_ref[:, j * D_MODEL:(j + 1) * D_MODEL])

    merged = gate(0) * jnp.dot(y_pool, wbp_ref[...], preferred_element_type=F32)
    merged = merged + gate(1) * jnp.dot(ys_ref[...], wbs_ref[...], preferred_element_type=F32)
    merged = merged + gate(2) * jnp.dot(y_mem, wbm_ref[...], preferred_element_type=F32)
    x1 = x_ref[...] + jnp.dot(merged.astype(BF16), wout_ref[...], preferred_element_type=F32)
    x1_ref[...] = x1

    h2 = _rms(x1, g2_ref[...]).astype(BF16)
    h2_ref[...] = h2
    for j in range(2 * PEER_HEADS):
        qp_ref[j] = jnp.dot(h2, wq_ref[:, j * PEER_HALF:(j + 1) * PEER_HALF],
                            preferred_element_type=F32).astype(BF16)


def _merge(x2, up, ys, qm, gl, km, vm, wpool, pscale, wbp, wbs, wbm, wout, g2, wq, seq):
    t = x2.shape[0]
    tm = TILE_MERGE
    tiles_per_seq = seq // tm
    halo_blocks = tm // MAX_WINDOW
    mem_len = km.shape[1]
    row = lambda w: pl.BlockSpec((tm, w), lambda i: (i, 0))
    memblk = pl.BlockSpec((1, mem_len, MEM_WIDTH), lambda i: (i // tiles_per_seq, 0, 0))
    return pl.pallas_call(
        functools.partial(_merge_body, tiles_per_seq),
        grid=(t // tm,),
        in_specs=[
            row(D_MODEL), row(POOL_WIDTH),
            pl.BlockSpec((MAX_WINDOW, POOL_WIDTH), lambda i: (jnp.maximum(i * halo_blocks - 1, 0), 0)),
            row(SWA_WIDTH), row(MEM_WIDTH), row(GATE_WIDTH), memblk, memblk,
            _const_spec((len(POOL_WINDOWS), POOL_GROUP_DIM, POOL_GROUP_DIM)),
            _const_spec((1, POOL_WIDTH)),
            _const_spec((POOL_WIDTH, D_MODEL)), _const_spec((SWA_WIDTH, D_MODEL)),
            _const_spec((MEM_WIDTH, D_MODEL)), _const_spec((D_MODEL, D_MODEL)),
            _const_spec((1, D_MODEL)), _const_spec((D_MODEL, PEER_Q_WIDTH)),
        ],
        out_specs=[row(D_MODEL), row(D_MODEL),
                   pl.BlockSpec((2 * PEER_HEADS, tm, PEER_HALF), lambda i: (0, i, 0))],
        out_shape=[
            jax.ShapeDtypeStruct((t, D_MODEL), F32),
            jax.ShapeDtypeStruct((t, D_MODEL), BF16),
            jax.ShapeDtypeStruct((2 * PEER_HEADS, t, PEER_HALF), BF16),
        ],
        compiler_params=pltpu.CompilerParams(
            dimension_semantics=("parallel",), vmem_limit_bytes=VMEM_LIMIT),
        name="merge",
    )(x2, up, up, ys, qm, gl, km, vm, wpool, pscale, wbp, wbs, wbm, wout, g2, wq)


def _topk_body(qp_ref, sk_ref, ids_ref, gates_ref):
    tt = qp_ref.shape[1]
    k = PEER_TOPK
    key_iota = lax.broadcasted_iota(I32, (N_KEYS, tt), 0)
    row_iota = lax.broadcasted_iota(I32, (k, tt), 0)

    def top16(sc, iota, n_rows, payload=None):
        vals = jnp.zeros((k, tt), F32)
        sel_rows = jnp.zeros((k, tt), I32)
        for i in range(k):
            m = jnp.max(sc, axis=0, keepdims=True)
            r = jnp.min(jnp.where(sc == m, iota, n_rows), axis=0, keepdims=True)
            hit = iota == r
            out = r if payload is None else jnp.max(jnp.where(hit, payload, -1), axis=0, keepdims=True)
            vals = jnp.where(row_iota == i, m, vals)
            sel_rows = jnp.where(row_iota == i, out, sel_rows)
            sc = jnp.where(hit, -jnp.inf, sc)
        return vals, sel_rows

    s0, i0 = top16(_dot_nt(sk_ref[0], qp_ref[0]), key_iota, N_KEYS)
    s1, i1 = top16(_dot_nt(sk_ref[1], qp_ref[1]), key_iota, N_KEYS)

    cand, cid = [], []
    for a in range(k):
        valid = row_iota < (k // (a + 1))
        cand.append(jnp.where(valid, s0[a:a + 1, :] + s1, -jnp.inf))
        cid.append(i0[a:a + 1, :] * N_KEYS + i1)
    cand = jnp.concatenate(cand, axis=0)
    cid = jnp.concatenate(cid, axis=0)
    pos_iota = lax.broadcasted_iota(I32, (k * k, tt), 0)
    s_fin, ids = top16(cand, pos_iota, k * k, payload=cid)

    e = jnp.exp(s_fin - s_fin[0:1, :])
    gates_ref[0] = e / jnp.sum(e, axis=0, keepdims=True)
    ids_ref[0] = ids


def _topk(qp, sk_bf):
    t = qp.shape[1]
    tt = TILE_TOPK
    out_blk = pl.BlockSpec((1, PEER_TOPK, tt), lambda i, h: (h, 0, i))
    return pl.pallas_call(
        _topk_body,
        grid=(t // tt, PEER_HEADS),
        in_specs=[
            pl.BlockSpec((2, tt, PEER_HALF), lambda i, h: (h, i, 0)),
            pl.BlockSpec((2, N_KEYS, PEER_HALF), lambda i, h: (h, 0, 0)),
        ],
        out_specs=[out_blk, out_blk],
        out_shape=[
            jax.ShapeDtypeStruct((PEER_HEADS, PEER_TOPK, t), I32),
            jax.ShapeDtypeStruct((PEER_HEADS, PEER_TOPK, t), F32),
        ],
        compiler_params=pltpu.CompilerParams(
            dimension_semantics=("parallel", "parallel"), vmem_limit_bytes=VMEM_LIMIT),
        name="peer_topk",
    )(qp, sk_bf)


def _gather_rows(table, indices, start, n):
    w = table.shape[1]
    per_worker = n // SC_WORKERS
    steps = per_worker // SC_GATHER_WINDOW
    assert per_worker * SC_WORKERS == n and steps * SC_GATHER_WINDOW == per_worker
    mesh = plsc.VectorSubcoreMesh(core_axis_name="core", subcore_axis_name="subcore")

    @functools.partial(
        pl.kernel,
        out_type=jax.ShapeDtypeStruct((n, w), table.dtype),
        mesh=mesh,
        scratch_types=[
            pltpu.VMEM((SC_GATHER_WINDOW,), I32),
            pltpu.VMEM((SC_GATHER_WINDOW, w), table.dtype),
            pltpu.SemaphoreType.DMA,
        ],
    )
    def gather_kernel(table_hbm, idx_hbm, out_hbm, idx_v, rows_v, sem):
        worker = lax.axis_index("subcore") * SC_CORES + lax.axis_index("core")
        base = worker * per_worker

        @pl.loop(0, steps)
        def _(s):
            off = pl.multiple_of(base + s * SC_GATHER_WINDOW, SC_GATHER_WINDOW)
            pltpu.sync_copy(idx_hbm.at[pl.ds(start + off, SC_GATHER_WINDOW)], idx_v)
            pltpu.async_copy(table_hbm.at[idx_v], rows_v, sem).wait()
            pltpu.sync_copy(rows_v, out_hbm.at[pl.ds(off, SC_GATHER_WINDOW)])

    return gather_kernel(table, indices)


def _pack_rows(tbl):
    b = lax.bitcast_convert_type(tbl.astype(BF16), jnp.uint16).astype(U32)
    return b[:, :PACKED_WIDTH] | (b[:, PACKED_WIDTH:] << 16)


def _unpack(words):
    lo = lax.bitcast_convert_type(words << 16, F32)
    hi = lax.bitcast_convert_type(words & jnp.uint32(0xFFFF0000), F32)
    return lo, hi


def _lane_sum_replicated(x, ones_bf):
    hi = x.astype(BF16)
    lo = (x - hi.astype(F32)).astype(BF16)
    return (jnp.dot(hi, ones_bf, preferred_element_type=F32)
            + jnp.dot(lo, ones_bf, preferred_element_type=F32))


def _expert_body(ug_ref, vg_ref, h2_ref, gate_ref, x1_ref, gf_ref, o_ref, y_ref, hf_ref):
    tt = h2_ref.shape[0]
    hf_ref[...] = h2_ref[...].astype(F32)
    ones_bf = jnp.ones((128, 128), BF16)
    eye = (lax.broadcasted_iota(I32, (PEER_PICKS, PEER_PICKS), 0)
           == lax.broadcasted_iota(I32, (PEER_PICKS, PEER_PICKS), 1)).astype(F32)

    def token(t, carry):
        r0 = pl.multiple_of(t * PEER_PICKS, PEER_PICKS)
        h = hf_ref[pl.ds(t, 1), :]
        ulo, uhi = _unpack(ug_ref[pl.ds(r0, PEER_PICKS), :])
        prod = ulo * h[:, :PACKED_WIDTH] + uhi * h[:, PACKED_WIDTH:]
        part = prod[:, 0:128] + prod[:, 128:256] + prod[:, 256:384] + prod[:, 384:512]
        a = _lane_sum_replicated(part, ones_bf)
        g = _lane_sum_replicated(eye * gate_ref[pl.ds(t, 1), :], ones_bf)
        w = g * jax.nn.gelu(a)
        vlo, vhi = _unpack(vg_ref[pl.ds(r0, PEER_PICKS), :])
        cols = [jnp.sum(w * half[:, j * 128:(j + 1) * 128], axis=0, keepdims=True)
                for half in (vlo, vhi) for j in range(4)]
        y_ref[pl.ds(t, 1), :] = jnp.concatenate(cols, axis=1)
        return carry

    lax.fori_loop(0, tt, token, 0)
    o_ref[...] = _rms(x1_ref[...] + y_ref[...], gf_ref[...])


def _expert(ug, vg, h2, gates, x1, gf, first_token):
    t = h2.shape[0]
    tt = TILE_EXPERT
    steps = ug.shape[0] // (tt * PEER_PICKS)
    first = first_token // tt
    rows = pl.BlockSpec((tt * PEER_PICKS, PACKED_WIDTH), lambda i: (i, 0))
    tok = lambda w: pl.BlockSpec((tt, w), lambda i: (first + i, 0))
    return pl.pallas_call(
        _expert_body,
        grid=(steps,),
        in_specs=[rows, rows, tok(D_MODEL), tok(PEER_PICKS), tok(D_MODEL), _const_spec((1, D_MODEL))],
        out_specs=tok(D_MODEL),
        out_shape=jax.ShapeDtypeStruct((t, D_MODEL), F32),
        scratch_shapes=[pltpu.VMEM((tt, D_MODEL), F32), pltpu.VMEM((tt, D_MODEL), F32)],
        input_output_aliases={4: 0},
        compiler_params=pltpu.CompilerParams(
            dimension_semantics=("parallel",), vmem_limit_bytes=VMEM_LIMIT),
        name="peer_expert",
    )(ug, vg, h2, gates, x1, gf)


def kernel(x, mem, positions, norm1_g, mem_norm_g, w_in, w_pool, pool_scale, attn_sinks, w_mem_kv, w_branch_pool, w_branch_swa, w_branch_mem, w_out, norm2_g, peer_w_q, peer_sub_keys, peer_u, peer_v, final_norm_g):
    batch, seq, d = x.shape
    t = batch * seq
    assert norm1_g.shape[0] == 1, "single-layer trunk"
    half = SWA_HEAD_DIM // 2
    inv = ROPE_THETA ** (-jnp.arange(half, dtype=F32) / half)
    invf = jnp.tile(inv, 128 // half).reshape(1, 128)
    pos_col = positions.reshape(t, 1)
    row = lambda v: v.reshape(1, -1)

    x2 = x.reshape(t, d)
    up, q, k, v, qm, gl = _inproj(x2, row(norm1_g[0]), w_in[0].astype(BF16))
    ys = _swa(q, k, v, pos_col, attn_sinks[0], invf, batch, seq)
    km, vm = _memkv(mem, row(mem_norm_g[0]), w_mem_kv[0].astype(BF16))
    x1, h2, qp = _merge(
        x2, up, ys, qm, gl, km, vm, w_pool[0].astype(BF16), row(pool_scale[0]),
        w_branch_pool[0].astype(BF16), w_branch_swa[0].astype(BF16),
        w_branch_mem[0].astype(BF16), w_out[0].astype(BF16), row(norm2_g[0]),
        peer_w_q[0].astype(BF16), seq)
    sk = peer_sub_keys[0].reshape(2 * PEER_HEADS, N_KEYS, PEER_HALF).astype(BF16)
    ids_t, gates_t = _topk(qp, sk)
    ids_flat = ids_t.reshape(PEER_PICKS, t).T.reshape(-1)
    gates = gates_t.reshape(PEER_PICKS, t).T
    u_pack = _pack_rows(peer_u[0])
    v_pack = _pack_rows(peer_v[0])
    gf = row(final_norm_g)
    tc = t // EXPERT_CHUNKS
    out = x1
    for c in range(EXPERT_CHUNKS):
        ug = _gather_rows(u_pack, ids_flat, c * tc * PEER_PICKS, tc * PEER_PICKS)
        vg = _gather_rows(v_pack, ids_flat, c * tc * PEER_PICKS, tc * PEER_PICKS)
        out = _expert(ug, vg, h2, gates, out, gf, c * tc)
    return out.reshape(batch, seq, d)
```

```python
import functools

import jax
import jax.numpy as jnp
from jax import lax
from jax.experimental import pallas as pl
from jax.experimental.pallas import tpu as pltpu
from jax.experimental.pallas import tpu_sc as plsc

F32 = jnp.float32
BF16 = jnp.bfloat16
I32 = jnp.int32
U32 = jnp.uint32

D_MODEL = 1024
EPS = 1e-6
NEG_INF = -1e30

POOL_WINDOWS = (2, 4, 8, 16)
POOL_GROUP_DIM = 128
POOL_WIDTH = 512
MAX_WINDOW = 16

SWA_HEAD_DIM = 64
SWA_HEADS = 16
SWA_WIDTH = 1024
SWA_KV_WIDTH = 128
SWA_BLOCK = 128
ROPE_THETA = 10000.0

MEM_HEADS = 4
MEM_HEAD_DIM = 128
MEM_WIDTH = 512
GATE_WIDTH = 3 * D_MODEL
IN_WIDTH = POOL_WIDTH + SWA_WIDTH + 2 * SWA_KV_WIDTH + MEM_WIDTH + GATE_WIDTH

PEER_HEADS = 8
N_KEYS = 128
PEER_HALF = 128
PEER_TOPK = 16
PEER_PICKS = PEER_HEADS * PEER_TOPK
PEER_Q_WIDTH = PEER_HEADS * 2 * PEER_HALF
PACKED_WIDTH = D_MODEL // 2

SC_CORES = 2
SC_SUBCORES = 16
SC_WORKERS = SC_CORES * SC_SUBCORES
SC_GATHER_WINDOW = 128

VMEM_LIMIT = 48 * 1024 * 1024

TILE_INPROJ = 256
TILE_MERGE = 256
TILE_TOPK = 256
TILE_EXPERT = 16
EXPERT_CHUNKS = 8


def _const_spec(shape):
    nd = len(shape)
    return pl.BlockSpec(shape, lambda *_: (0,) * nd, pipeline_mode=pl.Buffered(1))


def _rms(x, g):
    ms = jnp.mean(x * x, axis=-1, keepdims=True)
    return x * lax.rsqrt(ms + EPS) * g


def _dot_nt(a, b):
    return lax.dot_general(a, b, (((1,), (1,)), ((), ())), preferred_element_type=F32)


def _inproj_body(x_ref, g_ref, w_ref, up_ref, q_ref, k_ref, v_ref, qm_ref, gl_ref):
    h = _rms(x_ref[...], g_ref[...]).astype(BF16)

    def seg(a, b):
        return jnp.dot(h, w_ref[:, a:b], preferred_element_type=F32)

    o = 0
    up_ref[...] = seg(o, o + POOL_WIDTH)
    o += POOL_WIDTH
    q_ref[...] = seg(o, o + SWA_WIDTH)
    o += SWA_WIDTH
    k_ref[...] = seg(o, o + SWA_KV_WIDTH)
    o += SWA_KV_WIDTH
    v_ref[...] = seg(o, o + SWA_KV_WIDTH).astype(BF16)
    o += SWA_KV_WIDTH
    qm_ref[...] = seg(o, o + MEM_WIDTH).astype(BF16)
    o += MEM_WIDTH
    gl_ref[...] = seg(o, o + GATE_WIDTH)


def _inproj(x2, g, w_bf):
    t = x2.shape[0]
    tm = TILE_INPROJ
    row = lambda w: pl.BlockSpec((tm, w), lambda i: (i, 0))
    return pl.pallas_call(
        _inproj_body,
        grid=(t // tm,),
        in_specs=[row(D_MODEL), _const_spec((1, D_MODEL)), _const_spec((D_MODEL, IN_WIDTH))],
        out_specs=[row(POOL_WIDTH), row(SWA_WIDTH), row(SWA_KV_WIDTH), row(SWA_KV_WIDTH),
                   row(MEM_WIDTH), row(GATE_WIDTH)],
        out_shape=[
            jax.ShapeDtypeStruct((t, POOL_WIDTH), F32),
            jax.ShapeDtypeStruct((t, SWA_WIDTH), F32),
            jax.ShapeDtypeStruct((t, SWA_KV_WIDTH), F32),
            jax.ShapeDtypeStruct((t, SWA_KV_WIDTH), BF16),
            jax.ShapeDtypeStruct((t, MEM_WIDTH), BF16),
            jax.ShapeDtypeStruct((t, GATE_WIDTH), F32),
        ],
        compiler_params=pltpu.CompilerParams(
            dimension_semantics=("parallel",), vmem_limit_bytes=VMEM_LIMIT),
        name="inproj",
    )(x2, g, w_bf)


def _swa_body(sink_ref, q_ref, kc_ref, kp_ref, vc_ref, vp_ref, pc_ref, pp_ref, invf_ref, o_ref):
    n = pl.program_id(1)
    invf = invf_ref[...]
    lane = lax.broadcasted_iota(I32, (1, 2 * SWA_HEAD_DIM), 1)
    first_half = (lane % SWA_HEAD_DIM) < (SWA_HEAD_DIM // 2)
    lo_head = lane < SWA_HEAD_DIM

    def cos_sin(p_ref):
        ang = p_ref[...].astype(F32) * invf
        return jnp.cos(ang), jnp.sin(ang)

    def rope(x, cs):
        partner = jnp.where(first_half, -pltpu.roll(x, 96, 1), pltpu.roll(x, 32, 1))
        return x * cs[0] + partner * cs[1]

    cs_c = cos_sin(pc_ref)
    cs_p = cos_sin(pp_ref)
    k_all = jnp.concatenate([rope(kp_ref[...], cs_p), rope(kc_ref[...], cs_c)], axis=0)
    v_all = jnp.concatenate([vp_ref[...].astype(F32), vc_ref[...].astype(F32)], axis=0)

    def split(a):
        sw = pltpu.roll(a, SWA_HEAD_DIM, 1)
        zero = jnp.zeros_like(a)
        left = (jnp.where(lo_head, a, zero).astype(BF16), jnp.where(lo_head, sw, zero).astype(BF16))
        right = (jnp.where(lo_head, zero, sw).astype(BF16), jnp.where(lo_head, zero, a).astype(BF16))
        return left, right

    k_left, k_right = split(k_all)
    v_left, v_right = split(v_all)

    qi = lax.broadcasted_iota(I32, (SWA_BLOCK, 2 * SWA_BLOCK), 0)
    ki = lax.broadcasted_iota(I32, (SWA_BLOCK, 2 * SWA_BLOCK), 1)
    allowed = (ki > qi) & (ki <= qi + SWA_BLOCK) & ((ki >= SWA_BLOCK) | (n > 0))
    scale = SWA_HEAD_DIM ** -0.5

    for c in range(SWA_HEADS // 2):
        g = c // (SWA_HEADS // 4)
        qc = rope(q_ref[:, c * 128:(c + 1) * 128], cs_c).astype(BF16)
        acc = jnp.zeros((SWA_BLOCK, 128), F32)
        for par, (kx, vx) in enumerate(((k_left[g], v_left[g]), (k_right[g], v_right[g]))):
            sink = sink_ref[2 * c + par]
            s = _dot_nt(qc, kx) * scale
            s = jnp.where(allowed, s, NEG_INF)
            m = jnp.maximum(jnp.max(s, axis=-1, keepdims=True), sink)
            p = jnp.exp(s - m)
            den = jnp.sum(p, axis=-1, keepdims=True) + jnp.exp(sink - m)
            pn = (p / den).astype(BF16)
            acc = acc + jnp.dot(pn, vx, preferred_element_type=F32)
        o_ref[:, c * 128:(c + 1) * 128] = acc.astype(BF16)


def _swa(q, k, v, pos_col, sinks, invf, batch, seq):
    nb = seq // SWA_BLOCK
    t = batch * seq
    cur = lambda w: pl.BlockSpec((SWA_BLOCK, w), lambda b, n: (b * nb + n, 0))
    prev = lambda w: pl.BlockSpec((SWA_BLOCK, w), lambda b, n: (b * nb + jnp.maximum(n - 1, 0), 0))
    return pl.pallas_call(
        _swa_body,
        grid=(batch, nb),
        in_specs=[
            pl.BlockSpec(memory_space=pltpu.SMEM),
            cur(SWA_WIDTH), cur(SWA_KV_WIDTH), prev(SWA_KV_WIDTH),
            cur(SWA_KV_WIDTH), prev(SWA_KV_WIDTH), cur(1), prev(1),
            pl.BlockSpec((1, 128), lambda b, n: (0, 0)),
        ],
        out_specs=cur(SWA_WIDTH),
        out_shape=jax.ShapeDtypeStruct((t, SWA_WIDTH), BF16),
        compiler_params=pltpu.CompilerParams(
            dimension_semantics=("parallel", "parallel"), vmem_limit_bytes=VMEM_LIMIT),
        name="swa",
    )(sinks, q, k, k, v, v, pos_col, pos_col, invf)


def _memkv_body(mem_ref, g_ref, w_ref, km_ref, vm_ref):
    mn = _rms(mem_ref[0], g_ref[...]).astype(BF16)
    kv = jnp.dot(mn, w_ref[...], preferred_element_type=F32)
    km_ref[0] = kv[:, :MEM_WIDTH].astype(BF16)
    vm_ref[0] = kv[:, MEM_WIDTH:].astype(BF16)


def _memkv(mem, g, w_bf):
    b, m, _ = mem.shape
    blk = lambda w: pl.BlockSpec((1, m, w), lambda i: (i, 0, 0))
    return pl.pallas_call(
        _memkv_body,
        grid=(b,),
        in_specs=[blk(D_MODEL), _const_spec((1, D_MODEL)), _const_spec((D_MODEL, 2 * MEM_WIDTH))],
        out_specs=[blk(MEM_WIDTH), blk(MEM_WIDTH)],
        out_shape=[jax.ShapeDtypeStruct((b, m, MEM_WIDTH), BF16)] * 2,
        compiler_params=pltpu.CompilerParams(
            dimension_semantics=("parallel",), vmem_limit_bytes=VMEM_LIMIT),
        name="memkv",
    )(mem, g, w_bf)


def _merge_body(tiles_per_seq, x_ref, up_ref, halo_ref, ys_ref, qm_ref, gl_ref, km_ref, vm_ref,
                wpool_ref, pscale_ref, wbp_ref, wbs_ref, wbm_ref, wout_ref, g2_ref, wq_ref,
                x1_ref, h2_ref, qp_ref):
    tm = x_ref.shape[0]
    tile_in_seq = pl.program_id(0) % tiles_per_seq

    halo = jnp.where(tile_in_seq > 0, halo_ref[...], 0.0)
    t_in_seq = tile_in_seq * tm + lax.broadcasted_iota(I32, (tm, 1), 0)
    pooled = []
    for gi, w in enumerate(POOL_WINDOWS):
        cols = slice(gi * POOL_GROUP_DIM, (gi + 1) * POOL_GROUP_DIM)
        u = up_ref[:, cols]
        s = jnp.concatenate([halo[:, cols], u], axis=0)
        sh = 1
        while sh < w:
            s = s + pltpu.roll(s, sh, 0)
            sh *= 2
        cnt = jnp.minimum(t_in_seq + 1, w).astype(F32)
        pg = s[MAX_WINDOW:, :] / cnt - u
        mixed = jnp.dot(pg.astype(BF16), wpool_ref[gi], preferred_element_type=F32)
        pooled.append(mixed)
    y_pool = (jnp.concatenate(pooled, axis=1) * pscale_ref[...]).astype(BF16)

    mem_scale = MEM_HEAD_DIM ** -0.5
    y_mem = []
    for hh in range(MEM_HEADS):
        cols = slice(hh * MEM_HEAD_DIM, (hh + 1) * MEM_HEAD_DIM)
        s = _dot_nt(qm_ref[:, cols], km_ref[0, :, cols]) * mem_scale
        e = jnp.exp(s - jnp.max(s, axis=-1, keepdims=True))
        p = (e / jnp.sum(e, axis=-1, keepdims=True)).astype(BF16)
        y_mem.append(jnp.dot(p, vm_ref[0, :, cols], preferred_element_type=F32))
    y_mem = jnp.concatenate(y_mem, axis=1).astype(BF16)

    def gate(j):
        return jax.nn.sigmoid(gl_ref[:, j * D_MODEL:(j + 1) * D_MODEL])

    merged = gate(0) * jnp.dot(y_pool, wbp_ref[...], preferred_element_type=F32)
    merged = merged + gate(1) * jnp.dot(ys_ref[...], wbs_ref[...], preferred_element_type=F32)
    merged = merged + gate(2) * jnp.dot(y_mem, wbm_ref[...], preferred_element_type=F32)
    x1 = x_ref[...] + jnp.dot(merged.astype(BF16), wout_ref[...], preferred_element_type=F32)
    x1_ref[...] = x1

    h2 = _rms(x1, g2_ref[...]).astype(BF16)
    h2_ref[...] = h2
    for j in range(2 * PEER_HEADS):
        qp_ref[j] = jnp.dot(h2, wq_ref[:, j * PEER_HALF:(j + 1) * PEER_HALF],
                            preferred_element_type=F32).astype(BF16)


def _merge(x2, up, ys, qm, gl, km, vm, wpool, pscale, wbp, wbs, wbm, wout, g2, wq, seq):
    t = x2.shape[0]
    tm = TILE_MERGE
    tiles_per_seq = seq // tm
    halo_blocks = tm // MAX_WINDOW
    mem_len = km.shape[1]
    row = lambda w: pl.BlockSpec((tm, w), lambda i: (i, 0))
    memblk = pl.BlockSpec((1, mem_len, MEM_WIDTH), lambda i: (i // tiles_per_seq, 0, 0))
    return pl.pallas_call(
        functools.partial(_merge_body, tiles_per_seq),
        grid=(t // tm,),
        in_specs=[
            row(D_MODEL), row(POOL_WIDTH),
            pl.BlockSpec((MAX_WINDOW, POOL_WIDTH), lambda i: (jnp.maximum(i * halo_blocks - 1, 0), 0)),
            row(SWA_WIDTH), row(MEM_WIDTH), row(GATE_WIDTH), memblk, memblk,
            _const_spec((len(POOL_WINDOWS), POOL_GROUP_DIM, POOL_GROUP_DIM)),
            _const_spec((1, POOL_WIDTH)),
            _const_spec((POOL_WIDTH, D_MODEL)), _const_spec((SWA_WIDTH, D_MODEL)),
            _const_spec((MEM_WIDTH, D_MODEL)), _const_spec((D_MODEL, D_MODEL)),
            _const_spec((1, D_MODEL)), _const_spec((D_MODEL, PEER_Q_WIDTH)),
        ],
        out_specs=[row(D_MODEL), row(D_MODEL),
                   pl.BlockSpec((2 * PEER_HEADS, tm, PEER_HALF), lambda i: (0, i, 0))],
        out_shape=[
            jax.ShapeDtypeStruct((t, D_MODEL), F32),
            jax.ShapeDtypeStruct((t, D_MODEL), BF16),
            jax.ShapeDtypeStruct((2 * PEER_HEADS, t, PEER_HALF), BF16),
        ],
        compiler_params=pltpu.CompilerParams(
            dimension_semantics=("parallel",), vmem_limit_bytes=VMEM_LIMIT),
        name="merge",
    )(x2, up, up, ys, qm, gl, km, vm, wpool, pscale, wbp, wbs, wbm, wout, g2, wq)


def _topk_body(qp_ref, sk_ref, ids_ref, gates_ref):
    tt = qp_ref.shape[1]
    k = PEER_TOPK
    key_iota = lax.broadcasted_iota(I32, (N_KEYS, tt), 0)
    row_iota = lax.broadcasted_iota(I32, (k, tt), 0)

    def top16(sc, iota, n_rows, payload=None):
        vals = jnp.zeros((k, tt), F32)
        sel_rows = jnp.zeros((k, tt), I32)
        for i in range(k):
            m = jnp.max(sc, axis=0, keepdims=True)
            r = jnp.min(jnp.where(sc == m, iota, n_rows), axis=0, keepdims=True)
            hit = iota == r
            out = r if payload is None else jnp.max(jnp.where(hit, payload, -1), axis=0, keepdims=True)
            vals = jnp.where(row_iota == i, m, vals)
            sel_rows = jnp.where(row_iota == i, out, sel_rows)
            sc = jnp.where(hit, -jnp.inf, sc)
        return vals, sel_rows

    s0, i0 = top16(_dot_nt(sk_ref[0], qp_ref[0]), key_iota, N_KEYS)
    s1, i1 = top16(_dot_nt(sk_ref[1], qp_ref[1]), key_iota, N_KEYS)

    cand, cid = [], []
    for a in range(k):
        valid = row_iota < (k // (a + 1))
        cand.append(jnp.where(valid, s0[a:a + 1, :] + s1, -jnp.inf))
        cid.append(i0[a:a + 1, :] * N_KEYS + i1)
    cand = jnp.concatenate(cand, axis=0)
    cid = jnp.concatenate(cid, axis=0)
    pos_iota = lax.broadcasted_iota(I32, (k * k, tt), 0)
    s_fin, ids = top16(cand, pos_iota, k * k, payload=cid)

    e = jnp.exp(s_fin - s_fin[0:1, :])
    gates_ref[0] = e / jnp.sum(e, axis=0, keepdims=True)
    ids_ref[0] = ids


def _topk(qp, sk_bf):
    t = qp.shape[1]
    tt = TILE_TOPK
    out_blk = pl.BlockSpec((1, PEER_TOPK, tt), lambda i, h: (h, 0, i))
    return pl.pallas_call(
        _topk_body,
        grid=(t // tt, PEER_HEADS),
        in_specs=[
            pl.BlockSpec((2, tt, PEER_HALF), lambda i, h: (h, i, 0)),
            pl.BlockSpec((2, N_KEYS, PEER_HALF), lambda i, h: (h, 0, 0)),
        ],
        out_specs=[out_blk, out_blk],
        out_shape=[
            jax.ShapeDtypeStruct((PEER_HEADS, PEER_TOPK, t), I32),
            jax.ShapeDtypeStruct((PEER_HEADS, PEER_TOPK, t), F32),
        ],
        compiler_params=pltpu.CompilerParams(
            dimension_semantics=("parallel", "parallel"), vmem_limit_bytes=VMEM_LIMIT),
        name="peer_topk",
    )(qp, sk_bf)


def _gather_rows(table, indices, start, n):
    w = table.shape[1]
    per_worker = n // SC_WORKERS
    steps = per_worker // SC_GATHER_WINDOW
    assert per_worker * SC_WORKERS == n and steps * SC_GATHER_WINDOW == per_worker
    mesh = plsc.VectorSubcoreMesh(core_axis_name="core", subcore_axis_name="subcore")

    @functools.partial(
        pl.kernel,
        out_type=jax.ShapeDtypeStruct((n, w), table.dtype),
        mesh=mesh,
        scratch_types=[
            pltpu.VMEM((SC_GATHER_WINDOW,), I32),
            pltpu.VMEM((SC_GATHER_WINDOW, w), table.dtype),
            pltpu.SemaphoreType.DMA,
        ],
    )
    def gather_kernel(table_hbm, idx_hbm, out_hbm, idx_v, rows_v, sem):
        worker = lax.axis_index("subcore") * SC_CORES + lax.axis_index("core")
        base = worker * per_worker

        @pl.loop(0, steps)
        def _(s):
            off = pl.multiple_of(base + s * SC_GATHER_WINDOW, SC_GATHER_WINDOW)
            pltpu.sync_copy(idx_hbm.at[pl.ds(start + off, SC_GATHER_WINDOW)], idx_v)
            pltpu.async_copy(table_hbm.at[idx_v], rows_v, sem).wait()
            pltpu.sync_copy(rows_v, out_hbm.at[pl.ds(off, SC_GATHER_WINDOW)])

    return gather_kernel(table, indices)


def _pack_rows(tbl):
    b = lax.bitcast_convert_type(tbl.astype(BF16), jnp.uint16).astype(U32)
    return b[:, :PACKED_WIDTH] | (b[:, PACKED_WIDTH:] << 16)


def _unpack(words):
    lo = lax.bitcast_convert_type(words << 16, F32)
    hi = lax.bitcast_convert_type(words & jnp.uint32(0xFFFF0000), F32)
    return lo, hi


def _lane_sum_replicated(x, ones_bf):
    hi = x.astype(BF16)
    lo = (x - hi.astype(F32)).astype(BF16)
    return (jnp.dot(hi, ones_bf, preferred_element_type=F32)
            + jnp.dot(lo, ones_bf, preferred_element_type=F32))


def _expert_body(ug_ref, vg_ref, h2_ref, gate_ref, x1_ref, gf_ref, o_ref, y_ref, hf_ref, p_ref, w_ref):
    tt = h2_ref.shape[0]
    hf_ref[...] = h2_ref[...].astype(F32)

    def dot_partials(t, carry):
        r0 = pl.multiple_of(t * PEER_PICKS, PEER_PICKS)
        h = hf_ref[pl.ds(t, 1), :]
        ulo, uhi = _unpack(ug_ref[pl.ds(r0, PEER_PICKS), :])
        prod = ulo * h[:, :PACKED_WIDTH] + uhi * h[:, PACKED_WIDTH:]
        p_ref[pl.ds(r0, PEER_PICKS), :] = (prod[:, 0:128] + prod[:, 128:256]
                                           + prod[:, 256:384] + prod[:, 384:512])
        return carry

    lax.fori_loop(0, tt, dot_partials, 0)

    ones_bf = jnp.ones((128, 128), BF16)
    eye = (lax.broadcasted_iota(I32, (PEER_PICKS, PEER_PICKS), 0)
           == lax.broadcasted_iota(I32, (PEER_PICKS, PEER_PICKS), 1)).astype(F32)
    a = _lane_sum_replicated(p_ref[...], ones_bf)
    gate_diag = (gate_ref[...][:, None, :] * eye[None]).reshape(tt * PEER_PICKS, PEER_PICKS)
    g = _lane_sum_replicated(gate_diag, ones_bf)
    w_ref[...] = g * jax.nn.gelu(a)

    def weighted_rows(t, carry):
        r0 = pl.multiple_of(t * PEER_PICKS, PEER_PICKS)
        w = w_ref[pl.ds(r0, PEER_PICKS), :]
        vlo, vhi = _unpack(vg_ref[pl.ds(r0, PEER_PICKS), :])
        cols = [jnp.sum(w * half[:, j * 128:(j + 1) * 128], axis=0, keepdims=True)
                for half in (vlo, vhi) for j in range(4)]
        y_ref[pl.ds(t, 1), :] = jnp.concatenate(cols, axis=1)
        return carry

    lax.fori_loop(0, tt, weighted_rows, 0)
    o_ref[...] = _rms(x1_ref[...] + y_ref[...], gf_ref[...])


def _expert(ug, vg, h2, gates, x1, gf, first_token):
    t = h2.shape[0]
    tt = TILE_EXPERT
    steps = ug.shape[0] // (tt * PEER_PICKS)
    first = first_token // tt
    rows = pl.BlockSpec((tt * PEER_PICKS, PACKED_WIDTH), lambda i: (i, 0))
    tok = lambda w: pl.BlockSpec((tt, w), lambda i: (first + i, 0))
    return pl.pallas_call(
        _expert_body,
        grid=(steps,),
        in_specs=[rows, rows, tok(D_MODEL), tok(PEER_PICKS), tok(D_MODEL), _const_spec((1, D_MODEL))],
        out_specs=tok(D_MODEL),
        out_shape=jax.ShapeDtypeStruct((t, D_MODEL), F32),
        scratch_shapes=[pltpu.VMEM((tt, D_MODEL), F32), pltpu.VMEM((tt, D_MODEL), F32),
                        pltpu.VMEM((tt * PEER_PICKS, 128), F32), pltpu.VMEM((tt * PEER_PICKS, 128), F32)],
        input_output_aliases={4: 0},
        compiler_params=pltpu.CompilerParams(
            dimension_semantics=("parallel",), vmem_limit_bytes=VMEM_LIMIT),
        name="peer_expert",
    )(ug, vg, h2, gates, x1, gf)


def kernel(x, mem, positions, norm1_g, mem_norm_g, w_in, w_pool, pool_scale, attn_sinks, w_mem_kv, w_branch_pool, w_branch_swa, w_branch_mem, w_out, norm2_g, peer_w_q, peer_sub_keys, peer_u, peer_v, final_norm_g):
    batch, seq, d = x.shape
    t = batch * seq
    assert norm1_g.shape[0] == 1, "single-layer trunk"
    half = SWA_HEAD_DIM // 2
    inv = ROPE_THETA ** (-jnp.arange(half, dtype=F32) / half)
    invf = jnp.tile(inv, 128 // half).reshape(1, 128)
    pos_col = positions.reshape(t, 1)
    row = lambda v: v.reshape(1, -1)

    x2 = x.reshape(t, d)
    up, q, k, v, qm, gl = _inproj(x2, row(norm1_g[0]), w_in[0].astype(BF16))
    ys = _swa(q, k, v, pos_col, attn_sinks[0], invf, batch, seq)
    km, vm = _memkv(mem, row(mem_norm_g[0]), w_mem_kv[0].astype(BF16))
    x1, h2, qp = _merge(
        x2, up, ys, qm, gl, km, vm, w_pool[0].astype(BF16), row(pool_scale[0]),
        w_branch_pool[0].astype(BF16), w_branch_swa[0].astype(BF16),
        w_branch_mem[0].astype(BF16), w_out[0].astype(BF16), row(norm2_g[0]),
        peer_w_q[0].astype(BF16), seq)
    sk = peer_sub_keys[0].reshape(2 * PEER_HEADS, N_KEYS, PEER_HALF).astype(BF16)
    ids_t, gates_t = _topk(qp, sk)
    ids_flat = ids_t.reshape(PEER_PICKS, t).T.reshape(-1)
    gates = gates_t.reshape(PEER_PICKS, t).T
    u_pack = _pack_rows(peer_u[0])
    v_pack = _pack_rows(peer_v[0])
    gf = row(final_norm_g)
    tc = t // EXPERT_CHUNKS
    out = x1
    for c in range(EXPERT_CHUNKS):
        ug = _gather_rows(u_pack, ids_flat, c * tc * PEER_PICKS, tc * PEER_PICKS)
        vg = _gather_rows(v_pack, ids_flat, c * tc * PEER_PICKS, tc * PEER_PICKS)
        out = _expert(ug, vg, h2, gates, out, gf, c * tc)
    return out.reshape(batch, seq, d)
```

```python
import functools

import jax
import jax.numpy as jnp
from jax import lax
from jax.experimental import pallas as pl
from jax.experimental.pallas import tpu as pltpu
from jax.experimental.pallas import tpu_sc as plsc

F32 = jnp.float32
BF16 = jnp.bfloat16
I32 = jnp.int32
U32 = jnp.uint32

D_MODEL = 1024
EPS = 1e-6
NEG_INF = -1e30

POOL_WINDOWS = (2, 4, 8, 16)
POOL_GROUP_DIM = 128
POOL_WIDTH = 512
MAX_WINDOW = 16

SWA_HEAD_DIM = 64
SWA_HEADS = 16
SWA_WIDTH = 1024
SWA_KV_WIDTH = 128
SWA_BLOCK = 128
ROPE_THETA = 10000.0

MEM_HEADS = 4
MEM_HEAD_DIM = 128
MEM_WIDTH = 512
GATE_WIDTH = 3 * D_MODEL
IN_WIDTH = POOL_WIDTH + SWA_WIDTH + 2 * SWA_KV_WIDTH + MEM_WIDTH + GATE_WIDTH

PEER_HEADS = 8
N_KEYS = 128
PEER_HALF = 128
PEER_TOPK = 16
PEER_PICKS = PEER_HEADS * PEER_TOPK
PEER_Q_WIDTH = PEER_HEADS * 2 * PEER_HALF
PACKED_WIDTH = D_MODEL // 2

SC_CORES = 2
SC_SUBCORES = 16
SC_WORKERS = SC_CORES * SC_SUBCORES
SC_GATHER_WINDOW = 128

VMEM_LIMIT = 48 * 1024 * 1024

TILE_INPROJ = 256
TILE_MERGE = 256
TILE_TOPK = 256
TILE_EXPERT = 16


def _const_spec(shape):
    nd = len(shape)
    return pl.BlockSpec(shape, lambda *_: (0,) * nd, pipeline_mode=pl.Buffered(1))


def _rms(x, g):
    ms = jnp.mean(x * x, axis=-1, keepdims=True)
    return x * lax.rsqrt(ms + EPS) * g


def _dot_nt(a, b):
    return lax.dot_general(a, b, (((1,), (1,)), ((), ())), preferred_element_type=F32)


def _inproj_body(x_ref, g_ref, w_ref, up_ref, q_ref, k_ref, v_ref, qm_ref, gl_ref):
    h = _rms(x_ref[...], g_ref[...]).astype(BF16)

    def seg(a, b):
        return jnp.dot(h, w_ref[:, a:b], preferred_element_type=F32)

    o = 0
    up_ref[...] = seg(o, o + POOL_WIDTH)
    o += POOL_WIDTH
    q_ref[...] = seg(o, o + SWA_WIDTH)
    o += SWA_WIDTH
    k_ref[...] = seg(o, o + SWA_KV_WIDTH)
    o += SWA_KV_WIDTH
    v_ref[...] = seg(o, o + SWA_KV_WIDTH).astype(BF16)
    o += SWA_KV_WIDTH
    qm_ref[...] = seg(o, o + MEM_WIDTH).astype(BF16)
    o += MEM_WIDTH
    gl_ref[...] = seg(o, o + GATE_WIDTH)


def _inproj(x2, g, w_bf):
    t = x2.shape[0]
    tm = TILE_INPROJ
    row = lambda w: pl.BlockSpec((tm, w), lambda i: (i, 0))
    return pl.pallas_call(
        _inproj_body,
        grid=(t // tm,),
        in_specs=[row(D_MODEL), _const_spec((1, D_MODEL)), _const_spec((D_MODEL, IN_WIDTH))],
        out_specs=[row(POOL_WIDTH), row(SWA_WIDTH), row(SWA_KV_WIDTH), row(SWA_KV_WIDTH),
                   row(MEM_WIDTH), row(GATE_WIDTH)],
        out_shape=[
            jax.ShapeDtypeStruct((t, POOL_WIDTH), F32),
            jax.ShapeDtypeStruct((t, SWA_WIDTH), F32),
            jax.ShapeDtypeStruct((t, SWA_KV_WIDTH), F32),
            jax.ShapeDtypeStruct((t, SWA_KV_WIDTH), BF16),
            jax.ShapeDtypeStruct((t, MEM_WIDTH), BF16),
            jax.ShapeDtypeStruct((t, GATE_WIDTH), F32),
        ],
        compiler_params=pltpu.CompilerParams(
            dimension_semantics=("parallel",), vmem_limit_bytes=VMEM_LIMIT),
        name="inproj",
    )(x2, g, w_bf)


def _swa_body(sink_ref, q_ref, kc_ref, kp_ref, vc_ref, vp_ref, pc_ref, pp_ref, invf_ref, o_ref):
    n = pl.program_id(1)
    invf = invf_ref[...]
    lane = lax.broadcasted_iota(I32, (1, 2 * SWA_HEAD_DIM), 1)
    first_half = (lane % SWA_HEAD_DIM) < (SWA_HEAD_DIM // 2)
    lo_head = lane < SWA_HEAD_DIM

    def cos_sin(p_ref):
        ang = p_ref[...].astype(F32) * invf
        return jnp.cos(ang), jnp.sin(ang)

    def rope(x, cs):
        partner = jnp.where(first_half, -pltpu.roll(x, 96, 1), pltpu.roll(x, 32, 1))
        return x * cs[0] + partner * cs[1]

    cs_c = cos_sin(pc_ref)
    cs_p = cos_sin(pp_ref)
    k_all = jnp.concatenate([rope(kp_ref[...], cs_p), rope(kc_ref[...], cs_c)], axis=0)
    v_all = jnp.concatenate([vp_ref[...].astype(F32), vc_ref[...].astype(F32)], axis=0)

    def split(a):
        sw = pltpu.roll(a, SWA_HEAD_DIM, 1)
        zero = jnp.zeros_like(a)
        left = (jnp.where(lo_head, a, zero).astype(BF16), jnp.where(lo_head, sw, zero).astype(BF16))
        right = (jnp.where(lo_head, zero, sw).astype(BF16), jnp.where(lo_head, zero, a).astype(BF16))
        return left, right

    k_left, k_right = split(k_all)
    v_left, v_right = split(v_all)

    qi = lax.broadcasted_iota(I32, (SWA_BLOCK, 2 * SWA_BLOCK), 0)
    ki = lax.broadcasted_iota(I32, (SWA_BLOCK, 2 * SWA_BLOCK), 1)
    allowed = (ki > qi) & (ki <= qi + SWA_BLOCK) & ((ki >= SWA_BLOCK) | (n > 0))
    scale = SWA_HEAD_DIM ** -0.5

    for c in range(SWA_HEADS // 2):
        g = c // (SWA_HEADS // 4)
        qc = rope(q_ref[:, c * 128:(c + 1) * 128], cs_c).astype(BF16)
        acc = jnp.zeros((SWA_BLOCK, 128), F32)
        for par, (kx, vx) in enumerate(((k_left[g], v_left[g]), (k_right[g], v_right[g]))):
            sink = sink_ref[2 * c + par]
            s = _dot_nt(qc, kx) * scale
            s = jnp.where(allowed, s, NEG_INF)
            m = jnp.maximum(jnp.max(s, axis=-1, keepdims=True), sink)
            p = jnp.exp(s - m)
            den = jnp.sum(p, axis=-1, keepdims=True) + jnp.exp(sink - m)
            pn = (p / den).astype(BF16)
            acc = acc + jnp.dot(pn, vx, preferred_element_type=F32)
        o_ref[:, c * 128:(c + 1) * 128] = acc.astype(BF16)


def _swa(q, k, v, pos_col, sinks, invf, batch, seq):
    nb = seq // SWA_BLOCK
    t = batch * seq
    cur = lambda w: pl.BlockSpec((SWA_BLOCK, w), lambda b, n: (b * nb + n, 0))
    prev = lambda w: pl.BlockSpec((SWA_BLOCK, w), lambda b, n: (b * nb + jnp.maximum(n - 1, 0), 0))
    return pl.pallas_call(
        _swa_body,
        grid=(batch, nb),
        in_specs=[
            pl.BlockSpec(memory_space=pltpu.SMEM),
            cur(SWA_WIDTH), cur(SWA_KV_WIDTH), prev(SWA_KV_WIDTH),
            cur(SWA_KV_WIDTH), prev(SWA_KV_WIDTH), cur(1), prev(1),
            pl.BlockSpec((1, 128), lambda b, n: (0, 0)),
        ],
        out_specs=cur(SWA_WIDTH),
        out_shape=jax.ShapeDtypeStruct((t, SWA_WIDTH), BF16),
        compiler_params=pltpu.CompilerParams(
            dimension_semantics=("parallel", "parallel"), vmem_limit_bytes=VMEM_LIMIT),
        name="swa",
    )(sinks, q, k, k, v, v, pos_col, pos_col, invf)


def _memkv_body(mem_ref, g_ref, w_ref, km_ref, vm_ref):
    mn = _rms(mem_ref[0], g_ref[...]).astype(BF16)
    kv = jnp.dot(mn, w_ref[...], preferred_element_type=F32)
    km_ref[0] = kv[:, :MEM_WIDTH].astype(BF16)
    vm_ref[0] = kv[:, MEM_WIDTH:].astype(BF16)


def _memkv(mem, g, w_bf):
    b, m, _ = mem.shape
    blk = lambda w: pl.BlockSpec((1, m, w), lambda i: (i, 0, 0))
    return pl.pallas_call(
        _memkv_body,
        grid=(b,),
        in_specs=[blk(D_MODEL), _const_spec((1, D_MODEL)), _const_spec((D_MODEL, 2 * MEM_WIDTH))],
        out_specs=[blk(MEM_WIDTH), blk(MEM_WIDTH)],
        out_shape=[jax.ShapeDtypeStruct((b, m, MEM_WIDTH), BF16)] * 2,
        compiler_params=pltpu.CompilerParams(
            dimension_semantics=("parallel",), vmem_limit_bytes=VMEM_LIMIT),
        name="memkv",
    )(mem, g, w_bf)


def _merge_body(tiles_per_seq, x_ref, up_ref, halo_ref, ys_ref, qm_ref, gl_ref, km_ref, vm_ref,
                wpool_ref, pscale_ref, wbp_ref, wbs_ref, wbm_ref, wout_ref, g2_ref, wq_ref,
                x1_ref, h2_ref, qp_ref):
    tm = x_ref.shape[0]
    tile_in_seq = pl.program_id(0) % tiles_per_seq

    halo = jnp.where(tile_in_seq > 0, halo_ref[...], 0.0)
    t_in_seq = tile_in_seq * tm + lax.broadcasted_iota(I32, (tm, 1), 0)
    pooled = []
    for gi, w in enumerate(POOL_WINDOWS):
        cols = slice(gi * POOL_GROUP_DIM, (gi + 1) * POOL_GROUP_DIM)
        u = up_ref[:, cols]
        s = jnp.concatenate([halo[:, cols], u], axis=0)
        sh = 1
        while sh < w:
            s = s + pltpu.roll(s, sh, 0)
            sh *= 2
        cnt = jnp.minimum(t_in_seq + 1, w).astype(F32)
        pg = s[MAX_WINDOW:, :] / cnt - u
        mixed = jnp.dot(pg.astype(BF16), wpool_ref[gi], preferred_element_type=F32)
        pooled.append(mixed)
    y_pool = (jnp.concatenate(pooled, axis=1) * pscale_ref[...]).astype(BF16)

    mem_scale = MEM_HEAD_DIM ** -0.5
    y_mem = []
    for hh in range(MEM_HEADS):
        cols = slice(hh * MEM_HEAD_DIM, (hh + 1) * MEM_HEAD_DIM)
        s = _dot_nt(qm_ref[:, cols], km_ref[0, :, cols]) * mem_scale
        e = jnp.exp(s - jnp.max(s, axis=-1, keepdims=True))
        p = (e / jnp.sum(e, axis=-1, keepdims=True)).astype(BF16)
        y_mem.append(jnp.dot(p, vm_ref[0, :, cols], preferred_element_type=F32))
    y_mem = jnp.concatenate(y_mem, axis=1).astype(BF16)

    def gate(j):
        return jax.nn.sigmoid(gl_ref[:, j * D_MODEL:(j + 1) * D_MODEL])

    merged = gate(0) * jnp.dot(y_pool, wbp_ref[...], preferred_element_type=F32)
    merged = merged + gate(1) * jnp.dot(ys_ref[...], wbs_ref[...], preferred_element_type=F32)
    merged = merged + gate(2) * jnp.dot(y_mem, wbm_ref[...], preferred_element_type=F32)
    x1 = x_ref[...] + jnp.dot(merged.astype(BF16), wout_ref[...], preferred_element_type=F32)
    x1_ref[...] = x1

    h2 = _rms(x1, g2_ref[...]).astype(BF16)
    h2_ref[...] = h2
    for j in range(2 * PEER_HEADS):
        qp_ref[j] = jnp.dot(h2, wq_ref[:, j * PEER_HALF:(j + 1) * PEER_HALF],
                            preferred_element_type=F32).astype(BF16)


def _merge(x2, up, ys, qm, gl, km, vm, wpool, pscale, wbp, wbs, wbm, wout, g2, wq, seq):
    t = x2.shape[0]
    tm = TILE_MERGE
    tiles_per_seq = seq // tm
    halo_blocks = tm // MAX_WINDOW
    mem_len = km.shape[1]
    row = lambda w: pl.BlockSpec((tm, w), lambda i: (i, 0))
    memblk = pl.BlockSpec((1, mem_len, MEM_WIDTH), lambda i: (i // tiles_per_seq, 0, 0))
    return pl.pallas_call(
        functools.partial(_merge_body, tiles_per_seq),
        grid=(t // tm,),
        in_specs=[
            row(D_MODEL), row(POOL_WIDTH),
            pl.BlockSpec((MAX_WINDOW, POOL_WIDTH), lambda i: (jnp.maximum(i * halo_blocks - 1, 0), 0)),
            row(SWA_WIDTH), row(MEM_WIDTH), row(GATE_WIDTH), memblk, memblk,
            _const_spec((len(POOL_WINDOWS), POOL_GROUP_DIM, POOL_GROUP_DIM)),
            _const_spec((1, POOL_WIDTH)),
            _const_spec((POOL_WIDTH, D_MODEL)), _const_spec((SWA_WIDTH, D_MODEL)),
            _const_spec((MEM_WIDTH, D_MODEL)), _const_spec((D_MODEL, D_MODEL)),
            _const_spec((1, D_MODEL)), _const_spec((D_MODEL, PEER_Q_WIDTH)),
        ],
        out_specs=[row(D_MODEL), row(D_MODEL),
                   pl.BlockSpec((2 * PEER_HEADS, tm, PEER_HALF), lambda i: (0, i, 0))],
        out_shape=[
            jax.ShapeDtypeStruct((t, D_MODEL), F32),
            jax.ShapeDtypeStruct((t, D_MODEL), BF16),
            jax.ShapeDtypeStruct((2 * PEER_HEADS, t, PEER_HALF), BF16),
        ],
        compiler_params=pltpu.CompilerParams(
            dimension_semantics=("parallel",), vmem_limit_bytes=VMEM_LIMIT),
        name="merge",
    )(x2, up, up, ys, qm, gl, km, vm, wpool, pscale, wbp, wbs, wbm, wout, g2, wq)


def _topk_body(qp_ref, sk_ref, ids_ref, gates_ref):
    tt = qp_ref.shape[1]
    k = PEER_TOPK
    key_iota = lax.broadcasted_iota(I32, (N_KEYS, tt), 0)
    row_iota = lax.broadcasted_iota(I32, (k, tt), 0)

    def top16(sc, iota, n_rows, payload=None):
        vals = jnp.zeros((k, tt), F32)
        sel_rows = jnp.zeros((k, tt), I32)
        for i in range(k):
            m = jnp.max(sc, axis=0, keepdims=True)
            r = jnp.min(jnp.where(sc == m, iota, n_rows), axis=0, keepdims=True)
            hit = iota == r
            out = r if payload is None else jnp.max(jnp.where(hit, payload, -1), axis=0, keepdims=True)
            vals = jnp.where(row_iota == i, m, vals)
            sel_rows = jnp.where(row_iota == i, out, sel_rows)
            sc = jnp.where(hit, -jnp.inf, sc)
        return vals, sel_rows

    s0, i0 = top16(_dot_nt(sk_ref[0], qp_ref[0]), key_iota, N_KEYS)
    s1, i1 = top16(_dot_nt(sk_ref[1], qp_ref[1]), key_iota, N_KEYS)

    cand, cid = [], []
    for a in range(k):
        valid = row_iota < (k // (a + 1))
        cand.append(jnp.where(valid, s0[a:a + 1, :] + s1, -jnp.inf))
        cid.append(i0[a:a + 1, :] * N_KEYS + i1)
    cand = jnp.concatenate(cand, axis=0)
    cid = jnp.concatenate(cid, axis=0)
    pos_iota = lax.broadcasted_iota(I32, (k * k, tt), 0)
    s_fin, ids = top16(cand, pos_iota, k * k, payload=cid)

    e = jnp.exp(s_fin - s_fin[0:1, :])
    gates_ref[0] = e / jnp.sum(e, axis=0, keepdims=True)
    ids_ref[0] = ids


def _topk(qp, sk_bf):
    t = qp.shape[1]
    tt = TILE_TOPK
    out_blk = pl.BlockSpec((1, PEER_TOPK, tt), lambda i, h: (h, 0, i))
    return pl.pallas_call(
        _topk_body,
        grid=(t // tt, PEER_HEADS),
        in_specs=[
            pl.BlockSpec((2, tt, PEER_HALF), lambda i, h: (h, i, 0)),
            pl.BlockSpec((2, N_KEYS, PEER_HALF), lambda i, h: (h, 0, 0)),
        ],
        out_specs=[out_blk, out_blk],
        out_shape=[
            jax.ShapeDtypeStruct((PEER_HEADS, PEER_TOPK, t), I32),
            jax.ShapeDtypeStruct((PEER_HEADS, PEER_TOPK, t), F32),
        ],
        compiler_params=pltpu.CompilerParams(
            dimension_semantics=("parallel", "parallel"), vmem_limit_bytes=VMEM_LIMIT),
        name="peer_topk",
    )(qp, sk_bf)


def _gather_rows(table, indices):
    w = table.shape[1]
    n = indices.size
    steps = indices.shape[0] // SC_WORKERS
    per_worker = steps * SC_GATHER_WINDOW
    assert indices.shape[1] == SC_GATHER_WINDOW and per_worker * SC_WORKERS == n
    mesh = plsc.VectorSubcoreMesh(core_axis_name="core", subcore_axis_name="subcore")

    @functools.partial(
        pl.kernel,
        out_type=jax.ShapeDtypeStruct((n, w), table.dtype),
        mesh=mesh,
        scratch_types=[
            pltpu.VMEM((steps, SC_GATHER_WINDOW), I32),
            pltpu.VMEM((SC_GATHER_WINDOW, w), table.dtype),
            pltpu.SemaphoreType.DMA,
        ],
    )
    def gather_kernel(table_hbm, idx_hbm, out_hbm, idx_v, rows_v, sem):
        worker = lax.axis_index("subcore") * SC_CORES + lax.axis_index("core")
        pltpu.sync_copy(idx_hbm.at[pl.ds(worker * steps, steps)], idx_v)

        @pl.loop(0, steps)
        def _(s):
            off = pl.multiple_of(worker * per_worker + s * SC_GATHER_WINDOW, SC_GATHER_WINDOW)
            pltpu.async_copy(table_hbm.at[idx_v.at[s]], rows_v, sem).wait()
            pltpu.sync_copy(rows_v, out_hbm.at[pl.ds(off, SC_GATHER_WINDOW)])

    return gather_kernel(table, indices)


def _pack_rows(tbl):
    b = lax.bitcast_convert_type(tbl.astype(BF16), jnp.uint16).astype(U32)
    return b[:, :PACKED_WIDTH] | (b[:, PACKED_WIDTH:] << 16)


def _unpack(words):
    lo = lax.bitcast_convert_type(words << 16, F32)
    hi = lax.bitcast_convert_type(words & jnp.uint32(0xFFFF0000), F32)
    return lo, hi


def _lane_sum_replicated(x, ones_bf):
    hi = x.astype(BF16)
    lo = (x - hi.astype(F32)).astype(BF16)
    return (jnp.dot(hi, ones_bf, preferred_element_type=F32)
            + jnp.dot(lo, ones_bf, preferred_element_type=F32))


def _expert_body(ug_ref, vg_ref, h2_ref, gate_ref, x1_ref, gf_ref, o_ref, y_ref, hf_ref, p_ref, w_ref):
    tt = h2_ref.shape[0]
    hf_ref[...] = h2_ref[...].astype(F32)

    def dot_partials(t, carry):
        r0 = pl.multiple_of(t * PEER_PICKS, PEER_PICKS)
        h = hf_ref[pl.ds(t, 1), :]
        ulo, uhi = _unpack(ug_ref[pl.ds(r0, PEER_PICKS), :])
        prod = ulo * h[:, :PACKED_WIDTH] + uhi * h[:, PACKED_WIDTH:]
        p_ref[pl.ds(r0, PEER_PICKS), :] = (prod[:, 0:128] + prod[:, 128:256]
                                           + prod[:, 256:384] + prod[:, 384:512])
        return carry

    lax.fori_loop(0, tt, dot_partials, 0)

    ones_bf = jnp.ones((128, 128), BF16)
    eye = (lax.broadcasted_iota(I32, (PEER_PICKS, PEER_PICKS), 0)
           == lax.broadcasted_iota(I32, (PEER_PICKS, PEER_PICKS), 1)).astype(F32)
    a = _lane_sum_replicated(p_ref[...], ones_bf)
    gate_diag = (gate_ref[...][:, None, :] * eye[None]).reshape(tt * PEER_PICKS, PEER_PICKS)
    g = _lane_sum_replicated(gate_diag, ones_bf)
    w_ref[...] = g * jax.nn.gelu(a)

    def weighted_rows(t, carry):
        r0 = pl.multiple_of(t * PEER_PICKS, PEER_PICKS)
        w = w_ref[pl.ds(r0, PEER_PICKS), :]
        vlo, vhi = _unpack(vg_ref[pl.ds(r0, PEER_PICKS), :])
        cols = [jnp.sum(w * half[:, j * 128:(j + 1) * 128], axis=0, keepdims=True)
                for half in (vlo, vhi) for j in range(4)]
        y_ref[pl.ds(t, 1), :] = jnp.concatenate(cols, axis=1)
        return carry

    lax.fori_loop(0, tt, weighted_rows, 0)
    o_ref[...] = _rms(x1_ref[...] + y_ref[...], gf_ref[...])


def _expert(ug, vg, h2, gates, x1, gf, first_token):
    t = h2.shape[0]
    tt = TILE_EXPERT
    steps = ug.shape[0] // (tt * PEER_PICKS)
    first = first_token // tt
    rows = pl.BlockSpec((tt * PEER_PICKS, PACKED_WIDTH), lambda i: (i, 0))
    tok = lambda w: pl.BlockSpec((tt, w), lambda i: (first + i, 0))
    return pl.pallas_call(
        _expert_body,
        grid=(steps,),
        in_specs=[rows, rows, tok(D_MODEL), tok(PEER_PICKS), tok(D_MODEL), _const_spec((1, D_MODEL))],
        out_specs=tok(D_MODEL),
        out_shape=jax.ShapeDtypeStruct((t, D_MODEL), F32),
        scratch_shapes=[pltpu.VMEM((tt, D_MODEL), F32), pltpu.VMEM((tt, D_MODEL), F32),
                        pltpu.VMEM((tt * PEER_PICKS, 128), F32), pltpu.VMEM((tt * PEER_PICKS, 128), F32)],
        input_output_aliases={4: 0},
        compiler_params=pltpu.CompilerParams(
            dimension_semantics=("parallel",), vmem_limit_bytes=VMEM_LIMIT),
        name="peer_expert",
    )(ug, vg, h2, gates, x1, gf)


def kernel(x, mem, positions, norm1_g, mem_norm_g, w_in, w_pool, pool_scale, attn_sinks, w_mem_kv, w_branch_pool, w_branch_swa, w_branch_mem, w_out, norm2_g, peer_w_q, peer_sub_keys, peer_u, peer_v, final_norm_g):
    batch, seq, d = x.shape
    assert norm1_g.shape[0] == 1, "single-layer trunk"
    half = SWA_HEAD_DIM // 2
    inv = ROPE_THETA ** (-jnp.arange(half, dtype=F32) / half)
    invf = jnp.tile(inv, 128 // half).reshape(1, 128)
    row = lambda v: v.reshape(1, -1)

    w_in_bf = w_in[0].astype(BF16)
    merge_weights = (
        w_pool[0].astype(BF16), row(pool_scale[0]), w_branch_pool[0].astype(BF16),
        w_branch_swa[0].astype(BF16), w_branch_mem[0].astype(BF16), w_out[0].astype(BF16),
        row(norm2_g[0]), peer_w_q[0].astype(BF16))
    sk = peer_sub_keys[0].reshape(2 * PEER_HEADS, N_KEYS, PEER_HALF).astype(BF16)
    u_pack = _pack_rows(peer_u[0])
    v_pack = _pack_rows(peer_v[0])
    gf = row(final_norm_g)
    km, vm = _memkv(mem, row(mem_norm_g[0]), w_mem_kv[0].astype(BF16))

    outs = []
    for b in range(batch):
        xb = x[b]
        up, q, k, v, qm, gl = _inproj(xb, row(norm1_g[0]), w_in_bf)
        ys = _swa(q, k, v, positions[b].reshape(seq, 1), attn_sinks[0], invf, 1, seq)
        x1, h2, qp = _merge(xb, up, ys, qm, gl, km[b:b + 1], vm[b:b + 1], *merge_weights, seq)
        ids_t, gates_t = _topk(qp, sk)
        ids = ids_t.reshape(PEER_PICKS, seq).T
        gates = gates_t.reshape(PEER_PICKS, seq).T
        ug = _gather_rows(u_pack, ids)
        vg = _gather_rows(v_pack, ids)
        outs.append(_expert(ug, vg, h2, gates, x1, gf, 0))
    return jnp.stack(outs, axis=0)
```

```python
import functools

import jax
import jax.numpy as jnp
from jax import lax
from jax.experimental import pallas as pl
from jax.experimental.pallas import tpu as pltpu
from jax.experimental.pallas import tpu_sc as plsc

F32 = jnp.float32
BF16 = jnp.bfloat16
I32 = jnp.int32
U32 = jnp.uint32

D_MODEL = 1024
EPS = 1e-6
NEG_INF = -1e30

POOL_WINDOWS = (2, 4, 8, 16)
POOL_GROUP_DIM = 128
POOL_WIDTH = 512
MAX_WINDOW = 16

SWA_HEAD_DIM = 64
SWA_HEADS = 16
SWA_WIDTH = 1024
SWA_KV_WIDTH = 128
SWA_BLOCK = 128
ROPE_THETA = 10000.0

MEM_HEADS = 4
MEM_HEAD_DIM = 128
MEM_WIDTH = 512
GATE_WIDTH = 3 * D_MODEL
IN_WIDTH = POOL_WIDTH + SWA_WIDTH + 2 * SWA_KV_WIDTH + MEM_WIDTH + GATE_WIDTH

PEER_HEADS = 8
N_KEYS = 128
PEER_HALF = 128
PEER_TOPK = 16
PEER_PICKS = PEER_HEADS * PEER_TOPK
PEER_Q_WIDTH = PEER_HEADS * 2 * PEER_HALF
PACKED_WIDTH = D_MODEL // 2

SC_CORES = 2
SC_SUBCORES = 16
SC_WORKERS = SC_CORES * SC_SUBCORES
SC_LANES = 16
SC_QUARTER = 32

VMEM_LIMIT = 48 * 1024 * 1024

TILE_INPROJ = 256
TILE_MERGE = 256
TILE_TOPK = 256
TILE_EXPERT = 16


def _const_spec(shape):
    nd = len(shape)
    return pl.BlockSpec(shape, lambda *_: (0,) * nd, pipeline_mode=pl.Buffered(1))


def _rms(x, g):
    ms = jnp.mean(x * x, axis=-1, keepdims=True)
    return x * lax.rsqrt(ms + EPS) * g


def _dot_nt(a, b):
    return lax.dot_general(a, b, (((1,), (1,)), ((), ())), preferred_element_type=F32)


def _inproj_body(x_ref, g_ref, w_ref, up_ref, q_ref, k_ref, v_ref, qm_ref, gl_ref):
    h = _rms(x_ref[...], g_ref[...]).astype(BF16)

    def seg(a, b):
        return jnp.dot(h, w_ref[:, a:b], preferred_element_type=F32)

    o = 0
    up_ref[...] = seg(o, o + POOL_WIDTH)
    o += POOL_WIDTH
    q_ref[...] = seg(o, o + SWA_WIDTH)
    o += SWA_WIDTH
    k_ref[...] = seg(o, o + SWA_KV_WIDTH)
    o += SWA_KV_WIDTH
    v_ref[...] = seg(o, o + SWA_KV_WIDTH).astype(BF16)
    o += SWA_KV_WIDTH
    qm_ref[...] = seg(o, o + MEM_WIDTH).astype(BF16)
    o += MEM_WIDTH
    gl_ref[...] = seg(o, o + GATE_WIDTH)


def _inproj(x2, g, w_bf):
    t = x2.shape[0]
    tm = TILE_INPROJ
    row = lambda w: pl.BlockSpec((tm, w), lambda i: (i, 0))
    return pl.pallas_call(
        _inproj_body,
        grid=(t // tm,),
        in_specs=[row(D_MODEL), _const_spec((1, D_MODEL)), _const_spec((D_MODEL, IN_WIDTH))],
        out_specs=[row(POOL_WIDTH), row(SWA_WIDTH), row(SWA_KV_WIDTH), row(SWA_KV_WIDTH),
                   row(MEM_WIDTH), row(GATE_WIDTH)],
        out_shape=[
            jax.ShapeDtypeStruct((t, POOL_WIDTH), F32),
            jax.ShapeDtypeStruct((t, SWA_WIDTH), F32),
            jax.ShapeDtypeStruct((t, SWA_KV_WIDTH), F32),
            jax.ShapeDtypeStruct((t, SWA_KV_WIDTH), BF16),
            jax.ShapeDtypeStruct((t, MEM_WIDTH), BF16),
            jax.ShapeDtypeStruct((t, GATE_WIDTH), F32),
        ],
        compiler_params=pltpu.CompilerParams(
            dimension_semantics=("parallel",), vmem_limit_bytes=VMEM_LIMIT),
        name="inproj",
    )(x2, g, w_bf)


def _swa_body(sink_ref, q_ref, kc_ref, kp_ref, vc_ref, vp_ref, pc_ref, pp_ref, invf_ref, o_ref):
    n = pl.program_id(1)
    invf = invf_ref[...]
    lane = lax.broadcasted_iota(I32, (1, 2 * SWA_HEAD_DIM), 1)
    first_half = (lane % SWA_HEAD_DIM) < (SWA_HEAD_DIM // 2)
    lo_head = lane < SWA_HEAD_DIM

    def cos_sin(p_ref):
        ang = p_ref[...].astype(F32) * invf
        return jnp.cos(ang), jnp.sin(ang)

    def rope(x, cs):
        partner = jnp.where(first_half, -pltpu.roll(x, 96, 1), pltpu.roll(x, 32, 1))
        return x * cs[0] + partner * cs[1]

    cs_c = cos_sin(pc_ref)
    cs_p = cos_sin(pp_ref)
    k_all = jnp.concatenate([rope(kp_ref[...], cs_p), rope(kc_ref[...], cs_c)], axis=0)
    v_all = jnp.concatenate([vp_ref[...].astype(F32), vc_ref[...].astype(F32)], axis=0)

    def split(a):
        sw = pltpu.roll(a, SWA_HEAD_DIM, 1)
        zero = jnp.zeros_like(a)
        left = (jnp.where(lo_head, a, zero).astype(BF16), jnp.where(lo_head, sw, zero).astype(BF16))
        right = (jnp.where(lo_head, zero, sw).astype(BF16), jnp.where(lo_head, zero, a).astype(BF16))
        return left, right

    k_left, k_right = split(k_all)
    v_left, v_right = split(v_all)

    qi = lax.broadcasted_iota(I32, (SWA_BLOCK, 2 * SWA_BLOCK), 0)
    ki = lax.broadcasted_iota(I32, (SWA_BLOCK, 2 * SWA_BLOCK), 1)
    allowed = (ki > qi) & (ki <= qi + SWA_BLOCK) & ((ki >= SWA_BLOCK) | (n > 0))
    scale = SWA_HEAD_DIM ** -0.5

    for c in range(SWA_HEADS // 2):
        g = c // (SWA_HEADS // 4)
        qc = rope(q_ref[:, c * 128:(c + 1) * 128], cs_c).astype(BF16)
        acc = jnp.zeros((SWA_BLOCK, 128), F32)
        for par, (kx, vx) in enumerate(((k_left[g], v_left[g]), (k_right[g], v_right[g]))):
            sink = sink_ref[2 * c + par]
            s = _dot_nt(qc, kx) * scale
            s = jnp.where(allowed, s, NEG_INF)
            m = jnp.maximum(jnp.max(s, axis=-1, keepdims=True), sink)
            p = jnp.exp(s - m)
            den = jnp.sum(p, axis=-1, keepdims=True) + jnp.exp(sink - m)
            pn = (p / den).astype(BF16)
            acc = acc + jnp.dot(pn, vx, preferred_element_type=F32)
        o_ref[:, c * 128:(c + 1) * 128] = acc.astype(BF16)


def _swa(q, k, v, pos_col, sinks, invf, batch, seq):
    nb = seq // SWA_BLOCK
    t = batch * seq
    cur = lambda w: pl.BlockSpec((SWA_BLOCK, w), lambda b, n: (b * nb + n, 0))
    prev = lambda w: pl.BlockSpec((SWA_BLOCK, w), lambda b, n: (b * nb + jnp.maximum(n - 1, 0), 0))
    return pl.pallas_call(
        _swa_body,
        grid=(batch, nb),
        in_specs=[
            pl.BlockSpec(memory_space=pltpu.SMEM),
            cur(SWA_WIDTH), cur(SWA_KV_WIDTH), prev(SWA_KV_WIDTH),
            cur(SWA_KV_WIDTH), prev(SWA_KV_WIDTH), cur(1), prev(1),
            pl.BlockSpec((1, 128), lambda b, n: (0, 0)),
        ],
        out_specs=cur(SWA_WIDTH),
        out_shape=jax.ShapeDtypeStruct((t, SWA_WIDTH), BF16),
        compiler_params=pltpu.CompilerParams(
            dimension_semantics=("parallel", "parallel"), vmem_limit_bytes=VMEM_LIMIT),
        name="swa",
    )(sinks, q, k, k, v, v, pos_col, pos_col, invf)


def _memkv_body(mem_ref, g_ref, w_ref, km_ref, vm_ref):
    mn = _rms(mem_ref[0], g_ref[...]).astype(BF16)
    kv = jnp.dot(mn, w_ref[...], preferred_element_type=F32)
    km_ref[0] = kv[:, :MEM_WIDTH].astype(BF16)
    vm_ref[0] = kv[:, MEM_WIDTH:].astype(BF16)


def _memkv(mem, g, w_bf):
    b, m, _ = mem.shape
    blk = lambda w: pl.BlockSpec((1, m, w), lambda i: (i, 0, 0))
    return pl.pallas_call(
        _memkv_body,
        grid=(b,),
        in_specs=[blk(D_MODEL), _const_spec((1, D_MODEL)), _const_spec((D_MODEL, 2 * MEM_WIDTH))],
        out_specs=[blk(MEM_WIDTH), blk(MEM_WIDTH)],
        out_shape=[jax.ShapeDtypeStruct((b, m, MEM_WIDTH), BF16)] * 2,
        compiler_params=pltpu.CompilerParams(
            dimension_semantics=("parallel",), vmem_limit_bytes=VMEM_LIMIT),
        name="memkv",
    )(mem, g, w_bf)


def _merge_body(tiles_per_seq, x_ref, up_ref, halo_ref, ys_ref, qm_ref, gl_ref, km_ref, vm_ref,
                wpool_ref, pscale_ref, wbp_ref, wbs_ref, wbm_ref, wout_ref, g2_ref, wq_ref,
                x1_ref, h2_ref, qp_ref):
    tm = x_ref.shape[0]
    tile_in_seq = pl.program_id(0) % tiles_per_seq

    halo = jnp.where(tile_in_seq > 0, halo_ref[...], 0.0)
    t_in_seq = tile_in_seq * tm + lax.broadcasted_iota(I32, (tm, 1), 0)
    pooled = []
    for gi, w in enumerate(POOL_WINDOWS):
        cols = slice(gi * POOL_GROUP_DIM, (gi + 1) * POOL_GROUP_DIM)
        u = up_ref[:, cols]
        s = jnp.concatenate([halo[:, cols], u], axis=0)
        sh = 1
        while sh < w:
            s = s + pltpu.roll(s, sh, 0)
            sh *= 2
        cnt = jnp.minimum(t_in_seq + 1, w).astype(F32)
        pg = s[MAX_WINDOW:, :] / cnt - u
        mixed = jnp.dot(pg.astype(BF16), wpool_ref[gi], preferred_element_type=F32)
        pooled.append(mixed)
    y_pool = (jnp.concatenate(pooled, axis=1) * pscale_ref[...]).astype(BF16)

    mem_scale = MEM_HEAD_DIM ** -0.5
    y_mem = []
    for hh in range(MEM_HEADS):
        cols = slice(hh * MEM_HEAD_DIM, (hh + 1) * MEM_HEAD_DIM)
        s = _dot_nt(qm_ref[:, cols], km_ref[0, :, cols]) * mem_scale
        e = jnp.exp(s - jnp.max(s, axis=-1, keepdims=True))
        p = (e / jnp.sum(e, axis=-1, keepdims=True)).astype(BF16)
        y_mem.append(jnp.dot(p, vm_ref[0, :, cols], preferred_element_type=F32))
    y_mem = jnp.concatenate(y_mem, axis=1).astype(BF16)

    def gate(j):
        return jax.nn.sigmoid(gl_ref[:, j * D_MODEL:(j + 1) * D_MODEL])

    merged = gate(0) * jnp.dot(y_pool, wbp_ref[...], preferred_element_type=F32)
    merged = merged + gate(1) * jnp.dot(ys_ref[...], wbs_ref[...], preferred_element_type=F32)
    merged = merged + gate(2) * jnp.dot(y_mem, wbm_ref[...], preferred_element_type=F32)
    x1 = x_ref[...] + jnp.dot(merged.astype(BF16), wout_ref[...], preferred_element_type=F32)
    x1_ref[...] = x1

    h2 = _rms(x1, g2_ref[...]).astype(BF16)
    h2_ref[...] = h2.astype(F32)
    for j in range(2 * PEER_HEADS):
        qp_ref[j] = jnp.dot(h2, wq_ref[:, j * PEER_HALF:(j + 1) * PEER_HALF],
                            preferred_element_type=F32).astype(BF16)


def _merge(x2, up, ys, qm, gl, km, vm, wpool, pscale, wbp, wbs, wbm, wout, g2, wq, seq):
    t = x2.shape[0]
    tm = TILE_MERGE
    tiles_per_seq = seq // tm
    halo_blocks = tm // MAX_WINDOW
    mem_len = km.shape[1]
    row = lambda w: pl.BlockSpec((tm, w), lambda i: (i, 0))
    memblk = pl.BlockSpec((1, mem_len, MEM_WIDTH), lambda i: (i // tiles_per_seq, 0, 0))
    return pl.pallas_call(
        functools.partial(_merge_body, tiles_per_seq),
        grid=(t // tm,),
        in_specs=[
            row(D_MODEL), row(POOL_WIDTH),
            pl.BlockSpec((MAX_WINDOW, POOL_WIDTH), lambda i: (jnp.maximum(i * halo_blocks - 1, 0), 0)),
            row(SWA_WIDTH), row(MEM_WIDTH), row(GATE_WIDTH), memblk, memblk,
            _const_spec((len(POOL_WINDOWS), POOL_GROUP_DIM, POOL_GROUP_DIM)),
            _const_spec((1, POOL_WIDTH)),
            _const_spec((POOL_WIDTH, D_MODEL)), _const_spec((SWA_WIDTH, D_MODEL)),
            _const_spec((MEM_WIDTH, D_MODEL)), _const_spec((D_MODEL, D_MODEL)),
            _const_spec((1, D_MODEL)), _const_spec((D_MODEL, PEER_Q_WIDTH)),
        ],
        out_specs=[row(D_MODEL), row(D_MODEL),
                   pl.BlockSpec((2 * PEER_HEADS, tm, PEER_HALF), lambda i: (0, i, 0))],
        out_shape=[
            jax.ShapeDtypeStruct((t, D_MODEL), F32),
            jax.ShapeDtypeStruct((t, D_MODEL), F32),
            jax.ShapeDtypeStruct((2 * PEER_HEADS, t, PEER_HALF), BF16),
        ],
        compiler_params=pltpu.CompilerParams(
            dimension_semantics=("parallel",), vmem_limit_bytes=VMEM_LIMIT),
        name="merge",
    )(x2, up, up, ys, qm, gl, km, vm, wpool, pscale, wbp, wbs, wbm, wout, g2, wq)


def _topk_body(qp_ref, sk_ref, ids_ref, gates_ref):
    tt = qp_ref.shape[1]
    k = PEER_TOPK
    key_iota = lax.broadcasted_iota(I32, (N_KEYS, tt), 0)
    row_iota = lax.broadcasted_iota(I32, (k, tt), 0)

    def top16(sc, iota, n_rows, payload=None):
        vals = jnp.zeros((k, tt), F32)
        sel_rows = jnp.zeros((k, tt), I32)
        for i in range(k):
            m = jnp.max(sc, axis=0, keepdims=True)
            r = jnp.min(jnp.where(sc == m, iota, n_rows), axis=0, keepdims=True)
            hit = iota == r
            out = r if payload is None else jnp.max(jnp.where(hit, payload, -1), axis=0, keepdims=True)
            vals = jnp.where(row_iota == i, m, vals)
            sel_rows = jnp.where(row_iota == i, out, sel_rows)
            sc = jnp.where(hit, -jnp.inf, sc)
        return vals, sel_rows

    s0, i0 = top16(_dot_nt(sk_ref[0], qp_ref[0]), key_iota, N_KEYS)
    s1, i1 = top16(_dot_nt(sk_ref[1], qp_ref[1]), key_iota, N_KEYS)

    cand, cid = [], []
    for a in range(k):
        valid = row_iota < (k // (a + 1))
        cand.append(jnp.where(valid, s0[a:a + 1, :] + s1, -jnp.inf))
        cid.append(i0[a:a + 1, :] * N_KEYS + i1)
    cand = jnp.concatenate(cand, axis=0)
    cid = jnp.concatenate(cid, axis=0)
    pos_iota = lax.broadcasted_iota(I32, (k * k, tt), 0)
    s_fin, ids = top16(cand, pos_iota, k * k, payload=cid)

    e = jnp.exp(s_fin - s_fin[0:1, :])
    gates_ref[0] = e / jnp.sum(e, axis=0, keepdims=True)
    ids_ref[0] = ids


def _topk(qp, sk_bf):
    t = qp.shape[1]
    tt = TILE_TOPK
    out_blk = pl.BlockSpec((1, PEER_TOPK, tt), lambda i, h: (h, 0, i))
    return pl.pallas_call(
        _topk_body,
        grid=(t // tt, PEER_HEADS),
        in_specs=[
            pl.BlockSpec((2, tt, PEER_HALF), lambda i, h: (h, i, 0)),
            pl.BlockSpec((2, N_KEYS, PEER_HALF), lambda i, h: (h, 0, 0)),
        ],
        out_specs=[out_blk, out_blk],
        out_shape=[
            jax.ShapeDtypeStruct((PEER_HEADS, PEER_TOPK, t), I32),
            jax.ShapeDtypeStruct((PEER_HEADS, PEER_TOPK, t), F32),
        ],
        compiler_params=pltpu.CompilerParams(
            dimension_semantics=("parallel", "parallel"), vmem_limit_bytes=VMEM_LIMIT),
        name="peer_topk",
    )(qp, sk_bf)


def _route_rows(u_tbl, v_tbl, ids, h):
    s = ids.shape[0]
    tpw = s // SC_WORKERS
    assert tpw * SC_WORKERS == s and ids.shape[1] == PEER_PICKS
    nq = PEER_PICKS // SC_QUARTER
    mesh = plsc.VectorSubcoreMesh(core_axis_name="core", subcore_axis_name="subcore")

    @functools.partial(
        pl.kernel,
        out_type=(jax.ShapeDtypeStruct((s, PEER_PICKS), F32),
                  jax.ShapeDtypeStruct((s * PEER_PICKS, PACKED_WIDTH), U32)),
        mesh=mesh,
        scratch_types=[
            pltpu.VMEM((tpw, PEER_PICKS), I32),
            pltpu.VMEM((2, SC_QUARTER, PACKED_WIDTH), U32),
            pltpu.VMEM((2, SC_QUARTER, PACKED_WIDTH), U32),
            pltpu.VMEM((2, D_MODEL), F32),
            pltpu.VMEM((tpw, PEER_PICKS), F32),
            pltpu.SemaphoreType.DMA((2,)),
            pltpu.SemaphoreType.DMA((2,)),
            pltpu.SemaphoreType.DMA((2,)),
            pltpu.SemaphoreType.DMA((2,)),
        ],
        compiler_params=pltpu.CompilerParams(needs_layout_passes=False),
    )
    def route_kernel(u_hbm, v_hbm, idx_hbm, h_hbm, a_hbm, vg_hbm,
                     idx_v, u_v, v_v, h_v, a_v, u_sem, v_sem, w_sem, h_sem):
        worker = lax.axis_index("subcore") * SC_CORES + lax.axis_index("core")
        tok0 = worker * tpw
        pltpu.sync_copy(idx_hbm.at[pl.ds(tok0, tpw)], idx_v)
        lane = lax.iota(I32, SC_LANES)

        def u_gather(t, q, b):
            idx = idx_v.at[t, pl.ds(q * SC_QUARTER, SC_QUARTER)]
            return pltpu.make_async_copy(u_hbm.at[idx], u_v.at[b], u_sem.at[b])

        def v_gather(t, q, b):
            idx = idx_v.at[t, pl.ds(q * SC_QUARTER, SC_QUARTER)]
            return pltpu.make_async_copy(v_hbm.at[idx], v_v.at[b], v_sem.at[b])

        def v_write(t, q, b):
            off = pl.multiple_of((tok0 + t) * PEER_PICKS + q * SC_QUARTER, SC_QUARTER)
            return pltpu.make_async_copy(v_v.at[b], vg_hbm.at[pl.ds(off, SC_QUARTER)], w_sem.at[b])

        def h_load(t, hb):
            return pltpu.make_async_copy(h_hbm.at[tok0 + t], h_v.at[hb], h_sem.at[hb])

        h_load(0, 0).start()
        u_gather(0, 0, 0).start()
        v_gather(0, 0, 0).start()

        @pl.loop(0, tpw)
        def _(t):
            hb = t % 2
            h_load(t, hb).wait()

            @pl.when(t + 1 < tpw)
            def _():
                h_load(t + 1, 1 - hb).start()

            @pl.loop(0, nq)
            def _(q):
                b = q % 2
                last_q = q == nq - 1
                tn = jnp.where(last_q, t + 1, t)
                qn = jnp.where(last_q, 0, q + 1)
                has_next = tn < tpw

                u_gather(t, q, b).wait()

                @pl.when(has_next)
                def _():
                    u_gather(tn, qn, 1 - b).start()

                v_gather(t, q, b).wait()
                v_write(t, q, b).start()

                @pl.when((t > 0) | (q > 0))
                def _():
                    v_write(t, q, 1 - b).wait()

                @pl.when(has_next)
                def _():
                    v_gather(tn, qn, 1 - b).start()

                for g in range(SC_QUARTER // SC_LANES):
                    def two_picks(j2, vec):
                        k0 = g * SC_LANES + j2 * 2
                        accs = [jnp.zeros((SC_LANES,), F32) for _ in range(2)]
                        for c in range(PACKED_WIDTH // SC_LANES):
                            h_lo = h_v[hb, pl.ds(SC_LANES * c, SC_LANES)]
                            h_hi = h_v[hb, pl.ds(PACKED_WIDTH + SC_LANES * c, SC_LANES)]
                            for p in range(2):
                                words = u_v[b, k0 + p, pl.ds(SC_LANES * c, SC_LANES)]
                                lo = lax.bitcast_convert_type(words << 16, F32)
                                hi = lax.bitcast_convert_type(words & jnp.uint32(0xFFFF0000), F32)
                                accs[p] = accs[p] + lo * h_lo + hi * h_hi
                        for p in range(2):
                            vec = jnp.where(lane == j2 * 2 + p, jnp.sum(accs[p]), vec)
                        return vec

                    vec = lax.fori_loop(0, SC_LANES // 2, two_picks, jnp.zeros((SC_LANES,), F32))
                    a_v[t, pl.ds(q * SC_QUARTER + g * SC_LANES, SC_LANES)] = vec

        v_write(tpw - 1, nq - 1, (nq - 1) % 2).wait()
        pltpu.sync_copy(a_v, a_hbm.at[pl.ds(tok0, tpw)])

    return route_kernel(u_tbl, v_tbl, ids, h)


def _pack_rows(tbl):
    b = lax.bitcast_convert_type(tbl.astype(BF16), jnp.uint16).astype(U32)
    return b[:, :PACKED_WIDTH] | (b[:, PACKED_WIDTH:] << 16)


def _unpack(words):
    lo = lax.bitcast_convert_type(words << 16, F32)
    hi = lax.bitcast_convert_type(words & jnp.uint32(0xFFFF0000), F32)
    return lo, hi


def _lane_sum_replicated(x, ones_bf):
    hi = x.astype(BF16)
    lo = (x - hi.astype(F32)).astype(BF16)
    return (jnp.dot(hi, ones_bf, preferred_element_type=F32)
            + jnp.dot(lo, ones_bf, preferred_element_type=F32))


def _expert_body(a_ref, vg_ref, gate_ref, x1_ref, gf_ref, o_ref, y_ref, w_ref):
    tt = a_ref.shape[0]

    ones_bf = jnp.ones((128, 128), BF16)
    eye = (lax.broadcasted_iota(I32, (PEER_PICKS, PEER_PICKS), 0)
           == lax.broadcasted_iota(I32, (PEER_PICKS, PEER_PICKS), 1)).astype(F32)
    w_row = gate_ref[...] * jax.nn.gelu(a_ref[...])
    w_diag = (w_row[:, None, :] * eye[None]).reshape(tt * PEER_PICKS, PEER_PICKS)
    w_ref[...] = _lane_sum_replicated(w_diag, ones_bf)

    def weighted_rows(t, carry):
        r0 = pl.multiple_of(t * PEER_PICKS, PEER_PICKS)
        w = w_ref[pl.ds(r0, PEER_PICKS), :]
        vlo, vhi = _unpack(vg_ref[pl.ds(r0, PEER_PICKS), :])
        cols = [jnp.sum(w * half[:, j * 128:(j + 1) * 128], axis=0, keepdims=True)
                for half in (vlo, vhi) for j in range(4)]
        y_ref[pl.ds(t, 1), :] = jnp.concatenate(cols, axis=1)
        return carry

    lax.fori_loop(0, tt, weighted_rows, 0)
    o_ref[...] = _rms(x1_ref[...] + y_ref[...], gf_ref[...])


def _expert(a, vg, gates, x1, gf):
    t = a.shape[0]
    tt = TILE_EXPERT
    rows = pl.BlockSpec((tt * PEER_PICKS, PACKED_WIDTH), lambda i: (i, 0))
    tok = lambda w: pl.BlockSpec((tt, w), lambda i: (i, 0))
    return pl.pallas_call(
        _expert_body,
        grid=(t // tt,),
        in_specs=[tok(PEER_PICKS), rows, tok(PEER_PICKS), tok(D_MODEL), _const_spec((1, D_MODEL))],
        out_specs=tok(D_MODEL),
        out_shape=jax.ShapeDtypeStruct((t, D_MODEL), F32),
        scratch_shapes=[pltpu.VMEM((tt, D_MODEL), F32), pltpu.VMEM((tt * PEER_PICKS, 128), F32)],
        input_output_aliases={3: 0},
        compiler_params=pltpu.CompilerParams(
            dimension_semantics=("parallel",), vmem_limit_bytes=VMEM_LIMIT),
        name="peer_expert",
    )(a, vg, gates, x1, gf)


def kernel(x, mem, positions, norm1_g, mem_norm_g, w_in, w_pool, pool_scale, attn_sinks, w_mem_kv, w_branch_pool, w_branch_swa, w_branch_mem, w_out, norm2_g, peer_w_q, peer_sub_keys, peer_u, peer_v, final_norm_g):
    batch, seq, d = x.shape
    assert norm1_g.shape[0] == 1, "single-layer trunk"
    half = SWA_HEAD_DIM // 2
    inv = ROPE_THETA ** (-jnp.arange(half, dtype=F32) / half)
    invf = jnp.tile(inv, 128 // half).reshape(1, 128)
    row = lambda v: v.reshape(1, -1)

    w_in_bf = w_in[0].astype(BF16)
    merge_weights = (
        w_pool[0].astype(BF16), row(pool_scale[0]), w_branch_pool[0].astype(BF16),
        w_branch_swa[0].astype(BF16), w_branch_mem[0].astype(BF16), w_out[0].astype(BF16),
        row(norm2_g[0]), peer_w_q[0].astype(BF16))
    sk = peer_sub_keys[0].reshape(2 * PEER_HEADS, N_KEYS, PEER_HALF).astype(BF16)
    u_pack = _pack_rows(peer_u[0])
    v_pack = _pack_rows(peer_v[0])
    gf = row(final_norm_g)
    km, vm = _memkv(mem, row(mem_norm_g[0]), w_mem_kv[0].astype(BF16))

    outs = []
    for b in range(batch):
        xb = x[b]
        up, q, k, v, qm, gl = _inproj(xb, row(norm1_g[0]), w_in_bf)
        ys = _swa(q, k, v, positions[b].reshape(seq, 1), attn_sinks[0], invf, 1, seq)
        x1, h2, qp = _merge(xb, up, ys, qm, gl, km[b:b + 1], vm[b:b + 1], *merge_weights, seq)
        ids_t, gates_t = _topk(qp, sk)
        ids = ids_t.reshape(PEER_PICKS, seq).T
        gates = gates_t.reshape(PEER_PICKS, seq).T
        a, vg = _route_rows(u_pack, v_pack, ids, h2)
        outs.append(_expert(a, vg, gates, x1, gf))
    return jnp.stack(outs, axis=0)
```

```python
import functools

import jax
import jax.numpy as jnp
from jax import lax
from jax.experimental import pallas as pl
from jax.experimental.pallas import tpu as pltpu
from jax.experimental.pallas import tpu_sc as plsc

F32 = jnp.float32
BF16 = jnp.bfloat16
I32 = jnp.int32
U32 = jnp.uint32

D_MODEL = 1024
EPS = 1e-6
NEG_INF = -1e30

POOL_WINDOWS = (2, 4, 8, 16)
POOL_GROUP_DIM = 128
POOL_WIDTH = 512
MAX_WINDOW = 16

SWA_HEAD_DIM = 64
SWA_HEADS = 16
SWA_WIDTH = 1024
SWA_KV_WIDTH = 128
SWA_BLOCK = 128
ROPE_THETA = 10000.0

MEM_HEADS = 4
MEM_HEAD_DIM = 128
MEM_WIDTH = 512
GATE_WIDTH = 3 * D_MODEL
IN_WIDTH = POOL_WIDTH + SWA_WIDTH + 2 * SWA_KV_WIDTH + MEM_WIDTH + GATE_WIDTH

PEER_HEADS = 8
N_KEYS = 128
PEER_HALF = 128
PEER_TOPK = 16
PEER_PICKS = PEER_HEADS * PEER_TOPK
PEER_Q_WIDTH = PEER_HEADS * 2 * PEER_HALF
PACKED_WIDTH = D_MODEL // 2

SC_CORES = 2
SC_SUBCORES = 16
SC_WORKERS = SC_CORES * SC_SUBCORES
SC_LANES = 16
SC_QUARTER = 32

VMEM_LIMIT = 48 * 1024 * 1024

TILE_INPROJ = 256
TILE_MERGE = 256
TILE_TOPK = 256
TILE_EXPERT = 16
ROUTE_SPLITS = 2


def _const_spec(shape):
    nd = len(shape)
    return pl.BlockSpec(shape, lambda *_: (0,) * nd, pipeline_mode=pl.Buffered(1))


def _rms(x, g):
    ms = jnp.mean(x * x, axis=-1, keepdims=True)
    return x * lax.rsqrt(ms + EPS) * g


def _dot_nt(a, b):
    return lax.dot_general(a, b, (((1,), (1,)), ((), ())), preferred_element_type=F32)


def _inproj_body(x_ref, g_ref, w_ref, up_ref, q_ref, k_ref, v_ref, qm_ref, gl_ref):
    h = _rms(x_ref[...], g_ref[...]).astype(BF16)

    def seg(a, b):
        return jnp.dot(h, w_ref[:, a:b], preferred_element_type=F32)

    o = 0
    up_ref[...] = seg(o, o + POOL_WIDTH)
    o += POOL_WIDTH
    q_ref[...] = seg(o, o + SWA_WIDTH)
    o += SWA_WIDTH
    k_ref[...] = seg(o, o + SWA_KV_WIDTH)
    o += SWA_KV_WIDTH
    v_ref[...] = seg(o, o + SWA_KV_WIDTH).astype(BF16)
    o += SWA_KV_WIDTH
    qm_ref[...] = seg(o, o + MEM_WIDTH).astype(BF16)
    o += MEM_WIDTH
    gl_ref[...] = seg(o, o + GATE_WIDTH)


def _inproj(x2, g, w_bf):
    t = x2.shape[0]
    tm = TILE_INPROJ
    row = lambda w: pl.BlockSpec((tm, w), lambda i: (i, 0))
    return pl.pallas_call(
        _inproj_body,
        grid=(t // tm,),
        in_specs=[row(D_MODEL), _const_spec((1, D_MODEL)), _const_spec((D_MODEL, IN_WIDTH))],
        out_specs=[row(POOL_WIDTH), row(SWA_WIDTH), row(SWA_KV_WIDTH), row(SWA_KV_WIDTH),
                   row(MEM_WIDTH), row(GATE_WIDTH)],
        out_shape=[
            jax.ShapeDtypeStruct((t, POOL_WIDTH), F32),
            jax.ShapeDtypeStruct((t, SWA_WIDTH), F32),
            jax.ShapeDtypeStruct((t, SWA_KV_WIDTH), F32),
            jax.ShapeDtypeStruct((t, SWA_KV_WIDTH), BF16),
            jax.ShapeDtypeStruct((t, MEM_WIDTH), BF16),
            jax.ShapeDtypeStruct((t, GATE_WIDTH), F32),
        ],
        compiler_params=pltpu.CompilerParams(
            dimension_semantics=("parallel",), vmem_limit_bytes=VMEM_LIMIT),
        name="inproj",
    )(x2, g, w_bf)


def _swa_body(sink_ref, q_ref, kc_ref, kp_ref, vc_ref, vp_ref, pc_ref, pp_ref, invf_ref, o_ref):
    n = pl.program_id(1)
    invf = invf_ref[...]
    lane = lax.broadcasted_iota(I32, (1, 2 * SWA_HEAD_DIM), 1)
    first_half = (lane % SWA_HEAD_DIM) < (SWA_HEAD_DIM // 2)
    lo_head = lane < SWA_HEAD_DIM

    def cos_sin(p_ref):
        ang = p_ref[...].astype(F32) * invf
        return jnp.cos(ang), jnp.sin(ang)

    def rope(x, cs):
        partner = jnp.where(first_half, -pltpu.roll(x, 96, 1), pltpu.roll(x, 32, 1))
        return x * cs[0] + partner * cs[1]

    cs_c = cos_sin(pc_ref)
    cs_p = cos_sin(pp_ref)
    k_all = jnp.concatenate([rope(kp_ref[...], cs_p), rope(kc_ref[...], cs_c)], axis=0)
    v_all = jnp.concatenate([vp_ref[...].astype(F32), vc_ref[...].astype(F32)], axis=0)

    def split(a):
        sw = pltpu.roll(a, SWA_HEAD_DIM, 1)
        zero = jnp.zeros_like(a)
        left = (jnp.where(lo_head, a, zero).astype(BF16), jnp.where(lo_head, sw, zero).astype(BF16))
        right = (jnp.where(lo_head, zero, sw).astype(BF16), jnp.where(lo_head, zero, a).astype(BF16))
        return left, right

    k_left, k_right = split(k_all)
    v_left, v_right = split(v_all)

    qi = lax.broadcasted_iota(I32, (SWA_BLOCK, 2 * SWA_BLOCK), 0)
    ki = lax.broadcasted_iota(I32, (SWA_BLOCK, 2 * SWA_BLOCK), 1)
    allowed = (ki > qi) & (ki <= qi + SWA_BLOCK) & ((ki >= SWA_BLOCK) | (n > 0))
    scale = SWA_HEAD_DIM ** -0.5

    for c in range(SWA_HEADS // 2):
        g = c // (SWA_HEADS // 4)
        qc = rope(q_ref[:, c * 128:(c + 1) * 128], cs_c).astype(BF16)
        acc = jnp.zeros((SWA_BLOCK, 128), F32)
        for par, (kx, vx) in enumerate(((k_left[g], v_left[g]), (k_right[g], v_right[g]))):
            sink = sink_ref[2 * c + par]
            s = _dot_nt(qc, kx) * scale
            s = jnp.where(allowed, s, NEG_INF)
            m = jnp.maximum(jnp.max(s, axis=-1, keepdims=True), sink)
            p = jnp.exp(s - m)
            den = jnp.sum(p, axis=-1, keepdims=True) + jnp.exp(sink - m)
            pn = (p / den).astype(BF16)
            acc = acc + jnp.dot(pn, vx, preferred_element_type=F32)
        o_ref[:, c * 128:(c + 1) * 128] = acc.astype(BF16)


def _swa(q, k, v, pos_col, sinks, invf, batch, seq):
    nb = seq // SWA_BLOCK
    t = batch * seq
    cur = lambda w: pl.BlockSpec((SWA_BLOCK, w), lambda b, n: (b * nb + n, 0))
    prev = lambda w: pl.BlockSpec((SWA_BLOCK, w), lambda b, n: (b * nb + jnp.maximum(n - 1, 0), 0))
    return pl.pallas_call(
        _swa_body,
        grid=(batch, nb),
        in_specs=[
            pl.BlockSpec(memory_space=pltpu.SMEM),
            cur(SWA_WIDTH), cur(SWA_KV_WIDTH), prev(SWA_KV_WIDTH),
            cur(SWA_KV_WIDTH), prev(SWA_KV_WIDTH), cur(1), prev(1),
            pl.BlockSpec((1, 128), lambda b, n: (0, 0)),
        ],
        out_specs=cur(SWA_WIDTH),
        out_shape=jax.ShapeDtypeStruct((t, SWA_WIDTH), BF16),
        compiler_params=pltpu.CompilerParams(
            dimension_semantics=("parallel", "parallel"), vmem_limit_bytes=VMEM_LIMIT),
        name="swa",
    )(sinks, q, k, k, v, v, pos_col, pos_col, invf)


def _memkv_body(mem_ref, g_ref, w_ref, km_ref, vm_ref):
    mn = _rms(mem_ref[0], g_ref[...]).astype(BF16)
    kv = jnp.dot(mn, w_ref[...], preferred_element_type=F32)
    km_ref[0] = kv[:, :MEM_WIDTH].astype(BF16)
    vm_ref[0] = kv[:, MEM_WIDTH:].astype(BF16)


def _memkv(mem, g, w_bf):
    b, m, _ = mem.shape
    blk = lambda w: pl.BlockSpec((1, m, w), lambda i: (i, 0, 0))
    return pl.pallas_call(
        _memkv_body,
        grid=(b,),
        in_specs=[blk(D_MODEL), _const_spec((1, D_MODEL)), _const_spec((D_MODEL, 2 * MEM_WIDTH))],
        out_specs=[blk(MEM_WIDTH), blk(MEM_WIDTH)],
        out_shape=[jax.ShapeDtypeStruct((b, m, MEM_WIDTH), BF16)] * 2,
        compiler_params=pltpu.CompilerParams(
            dimension_semantics=("parallel",), vmem_limit_bytes=VMEM_LIMIT),
        name="memkv",
    )(mem, g, w_bf)


def _merge_body(tiles_per_seq, x_ref, up_ref, halo_ref, ys_ref, qm_ref, gl_ref, km_ref, vm_ref,
                wpool_ref, pscale_ref, wbp_ref, wbs_ref, wbm_ref, wout_ref, g2_ref, wq_ref,
                x1_ref, h2_ref, qp_ref):
    tm = x_ref.shape[0]
    tile_in_seq = pl.program_id(0) % tiles_per_seq

    halo = jnp.where(tile_in_seq > 0, halo_ref[...], 0.0)
    t_in_seq = tile_in_seq * tm + lax.broadcasted_iota(I32, (tm, 1), 0)
    pooled = []
    for gi, w in enumerate(POOL_WINDOWS):
        cols = slice(gi * POOL_GROUP_DIM, (gi + 1) * POOL_GROUP_DIM)
        u = up_ref[:, cols]
        s = jnp.concatenate([halo[:, cols], u], axis=0)
        sh = 1
        while sh < w:
            s = s + pltpu.roll(s, sh, 0)
            sh *= 2
        cnt = jnp.minimum(t_in_seq + 1, w).astype(F32)
        pg = s[MAX_WINDOW:, :] / cnt - u
        mixed = jnp.dot(pg.astype(BF16), wpool_ref[gi], preferred_element_type=F32)
        pooled.append(mixed)
    y_pool = (jnp.concatenate(pooled, axis=1) * pscale_ref[...]).astype(BF16)

    mem_scale = MEM_HEAD_DIM ** -0.5
    y_mem = []
    for hh in range(MEM_HEADS):
        cols = slice(hh * MEM_HEAD_DIM, (hh + 1) * MEM_HEAD_DIM)
        s = _dot_nt(qm_ref[:, cols], km_ref[0, :, cols]) * mem_scale
        e = jnp.exp(s - jnp.max(s, axis=-1, keepdims=True))
        p = (e / jnp.sum(e, axis=-1, keepdims=True)).astype(BF16)
        y_mem.append(jnp.dot(p, vm_ref[0, :, cols], preferred_element_type=F32))
    y_mem = jnp.concatenate(y_mem, axis=1).astype(BF16)

    def gate(j):
        return jax.nn.sigmoid(gl_ref[:, j * D_MODEL:(j + 1) * D_MODEL])

    merged = gate(0) * jnp.dot(y_pool, wbp_ref[...], preferred_element_type=F32)
    merged = merged + gate(1) * jnp.dot(ys_ref[...], wbs_ref[...], preferred_element_type=F32)
    merged = merged + gate(2) * jnp.dot(y_mem, wbm_ref[...], preferred_element_type=F32)
    x1 = x_ref[...] + jnp.dot(merged.astype(BF16), wout_ref[...], preferred_element_type=F32)
    x1_ref[...] = x1

    h2 = _rms(x1, g2_ref[...]).astype(BF16)
    h2_ref[...] = h2.astype(F32)
    for j in range(2 * PEER_HEADS):
        qp_ref[j] = jnp.dot(h2, wq_ref[:, j * PEER_HALF:(j + 1) * PEER_HALF],
                            preferred_element_type=F32).astype(BF16)


def _merge(x2, up, ys, qm, gl, km, vm, wpool, pscale, wbp, wbs, wbm, wout, g2, wq, seq):
    t = x2.shape[0]
    tm = TILE_MERGE
    tiles_per_seq = seq // tm
    halo_blocks = tm // MAX_WINDOW
    mem_len = km.shape[1]
    row = lambda w: pl.BlockSpec((tm, w), lambda i: (i, 0))
    memblk = pl.BlockSpec((1, mem_len, MEM_WIDTH), lambda i: (i // tiles_per_seq, 0, 0))
    return pl.pallas_call(
        functools.partial(_merge_body, tiles_per_seq),
        grid=(t // tm,),
        in_specs=[
            row(D_MODEL), row(POOL_WIDTH),
            pl.BlockSpec((MAX_WINDOW, POOL_WIDTH), lambda i: (jnp.maximum(i * halo_blocks - 1, 0), 0)),
            row(SWA_WIDTH), row(MEM_WIDTH), row(GATE_WIDTH), memblk, memblk,
            _const_spec((len(POOL_WINDOWS), POOL_GROUP_DIM, POOL_GROUP_DIM)),
            _const_spec((1, POOL_WIDTH)),
            _const_spec((POOL_WIDTH, D_MODEL)), _const_spec((SWA_WIDTH, D_MODEL)),
            _const_spec((MEM_WIDTH, D_MODEL)), _const_spec((D_MODEL, D_MODEL)),
            _const_spec((1, D_MODEL)), _const_spec((D_MODEL, PEER_Q_WIDTH)),
        ],
        out_specs=[row(D_MODEL), row(D_MODEL),
                   pl.BlockSpec((2 * PEER_HEADS, tm, PEER_HALF), lambda i: (0, i, 0))],
        out_shape=[
            jax.ShapeDtypeStruct((t, D_MODEL), F32),
            jax.ShapeDtypeStruct((t, D_MODEL), F32),
            jax.ShapeDtypeStruct((2 * PEER_HEADS, t, PEER_HALF), BF16),
        ],
        compiler_params=pltpu.CompilerParams(
            dimension_semantics=("parallel",), vmem_limit_bytes=VMEM_LIMIT),
        name="merge",
    )(x2, up, up, ys, qm, gl, km, vm, wpool, pscale, wbp, wbs, wbm, wout, g2, wq)


def _topk_body(qp_ref, sk_ref, ids_ref, gates_ref):
    tt = qp_ref.shape[1]
    k = PEER_TOPK
    row_iota = lax.broadcasted_iota(I32, (k, tt), 0)
    sub_iota = lax.broadcasted_iota(I32, (8, tt), 0)

    def top16(sc, payload=None):
        n_rows = sc.shape[0]
        pos = lax.broadcasted_iota(I32, (n_rows, tt), 0).astype(F32)
        vals = jnp.zeros((k, tt), F32)
        picked = jnp.zeros((k, tt), F32)
        for i in range(k):
            m = jnp.max(sc, axis=0, keepdims=True)
            r = jnp.min(jnp.where(sc == m, pos, float(n_rows)), axis=0, keepdims=True)
            hit = pos == r
            out = r if payload is None else jnp.max(jnp.where(hit, payload, -1.0), axis=0, keepdims=True)
            vals = jnp.where(row_iota == i, m, vals)
            picked = jnp.where(row_iota == i, out, picked)
            sc = jnp.where(hit, -jnp.inf, sc)
        return vals, picked

    s0, i0 = top16(_dot_nt(sk_ref[0], qp_ref[0]))
    s1, i1 = top16(_dot_nt(sk_ref[1], qp_ref[1]))

    cand, cid = [], []
    for a in range(k // 2):
        n_valid = k // (a + 1)
        for piece in range((n_valid + 7) // 8):
            rows = slice(piece * 8, piece * 8 + 8)
            sums = s0[a:a + 1, :] + s1[rows, :]
            if n_valid - piece * 8 < 8:
                sums = jnp.where(sub_iota < n_valid - piece * 8, sums, -jnp.inf)
            cand.append(sums)
            cid.append(i0[a:a + 1, :] * float(N_KEYS) + i1[rows, :])
    cand.append(s0[k // 2:, :] + s1[0:1, :])
    cid.append(i0[k // 2:, :] * float(N_KEYS) + i1[0:1, :])
    s_fin, ids = top16(jnp.concatenate(cand, axis=0), payload=jnp.concatenate(cid, axis=0))

    e = jnp.exp(s_fin - s_fin[0:1, :])
    gates_ref[0] = e / jnp.sum(e, axis=0, keepdims=True)
    ids_ref[0] = ids.astype(I32)


def _topk(qp, sk_bf, first_token, n_tokens):
    tt = TILE_TOPK
    first = first_token // tt
    out_blk = pl.BlockSpec((1, PEER_TOPK, tt), lambda i, h: (h, 0, i))
    return pl.pallas_call(
        _topk_body,
        grid=(n_tokens // tt, PEER_HEADS),
        in_specs=[
            pl.BlockSpec((2, tt, PEER_HALF), lambda i, h: (h, first + i, 0)),
            pl.BlockSpec((2, N_KEYS, PEER_HALF), lambda i, h: (h, 0, 0)),
        ],
        out_specs=[out_blk, out_blk],
        out_shape=[
            jax.ShapeDtypeStruct((PEER_HEADS, PEER_TOPK, n_tokens), I32),
            jax.ShapeDtypeStruct((PEER_HEADS, PEER_TOPK, n_tokens), F32),
        ],
        compiler_params=pltpu.CompilerParams(
            dimension_semantics=("parallel", "parallel"), vmem_limit_bytes=VMEM_LIMIT),
        name="peer_topk",
    )(qp, sk_bf)


def _route_rows(u_tbl, v_tbl, ids, h, h_first):
    s = ids.shape[0]
    tpw = s // SC_WORKERS
    assert tpw * SC_WORKERS == s and ids.shape[1] == PEER_PICKS
    nq = PEER_PICKS // SC_QUARTER
    mesh = plsc.VectorSubcoreMesh(core_axis_name="core", subcore_axis_name="subcore")

    @functools.partial(
        pl.kernel,
        out_type=(jax.ShapeDtypeStruct((s, PEER_PICKS), F32),
                  jax.ShapeDtypeStruct((s * PEER_PICKS, PACKED_WIDTH), U32)),
        mesh=mesh,
        scratch_types=[
            pltpu.VMEM((tpw, PEER_PICKS), I32),
            pltpu.VMEM((2, SC_QUARTER, PACKED_WIDTH), U32),
            pltpu.VMEM((2, SC_QUARTER, PACKED_WIDTH), U32),
            pltpu.VMEM((2, D_MODEL), F32),
            pltpu.VMEM((tpw, PEER_PICKS), F32),
            pltpu.SemaphoreType.DMA((2,)),
            pltpu.SemaphoreType.DMA((2,)),
            pltpu.SemaphoreType.DMA((2,)),
            pltpu.SemaphoreType.DMA((2,)),
        ],
        compiler_params=pltpu.CompilerParams(needs_layout_passes=False),
    )
    def route_kernel(u_hbm, v_hbm, idx_hbm, h_hbm, a_hbm, vg_hbm,
                     idx_v, u_v, v_v, h_v, a_v, u_sem, v_sem, w_sem, h_sem):
        worker = lax.axis_index("subcore") * SC_CORES + lax.axis_index("core")
        tok0 = worker * tpw
        pltpu.sync_copy(idx_hbm.at[pl.ds(tok0, tpw)], idx_v)
        lane = lax.iota(I32, SC_LANES)

        def u_gather(t, q, b):
            idx = idx_v.at[t, pl.ds(q * SC_QUARTER, SC_QUARTER)]
            return pltpu.make_async_copy(u_hbm.at[idx], u_v.at[b], u_sem.at[b])

        def v_gather(t, q, b):
            idx = idx_v.at[t, pl.ds(q * SC_QUARTER, SC_QUARTER)]
            return pltpu.make_async_copy(v_hbm.at[idx], v_v.at[b], v_sem.at[b])

        def v_write(t, q, b):
            off = pl.multiple_of((tok0 + t) * PEER_PICKS + q * SC_QUARTER, SC_QUARTER)
            return pltpu.make_async_copy(v_v.at[b], vg_hbm.at[pl.ds(off, SC_QUARTER)], w_sem.at[b])

        def h_load(t, hb):
            return pltpu.make_async_copy(h_hbm.at[h_first + tok0 + t], h_v.at[hb], h_sem.at[hb])

        h_load(0, 0).start()
        u_gather(0, 0, 0).start()
        v_gather(0, 0, 0).start()

        @pl.loop(0, tpw)
        def _(t):
            hb = t % 2
            h_load(t, hb).wait()

            @pl.when(t + 1 < tpw)
            def _():
                h_load(t + 1, 1 - hb).start()

            @pl.loop(0, nq)
            def _(q):
                b = q % 2
                last_q = q == nq - 1
                tn = jnp.where(last_q, t + 1, t)
                qn = jnp.where(last_q, 0, q + 1)
                has_next = tn < tpw

                u_gather(t, q, b).wait()

                @pl.when(has_next)
                def _():
                    u_gather(tn, qn, 1 - b).start()

                v_gather(t, q, b).wait()
                v_write(t, q, b).start()

                @pl.when((t > 0) | (q > 0))
                def _():
                    v_write(t, q, 1 - b).wait()

                @pl.when(has_next)
                def _():
                    v_gather(tn, qn, 1 - b).start()

                for g in range(SC_QUARTER // SC_LANES):
                    def two_picks(j2, vec):
                        k0 = g * SC_LANES + j2 * 2
                        accs = [jnp.zeros((SC_LANES,), F32) for _ in range(2)]
                        for c in range(PACKED_WIDTH // SC_LANES):
                            h_lo = h_v[hb, pl.ds(SC_LANES * c, SC_LANES)]
                            h_hi = h_v[hb, pl.ds(PACKED_WIDTH + SC_LANES * c, SC_LANES)]
                            for p in range(2):
                                words = u_v[b, k0 + p, pl.ds(SC_LANES * c, SC_LANES)]
                                lo = lax.bitcast_convert_type(words << 16, F32)
                                hi = lax.bitcast_convert_type(words & jnp.uint32(0xFFFF0000), F32)
                                accs[p] = accs[p] + lo * h_lo + hi * h_hi
                        for p in range(2):
                            vec = jnp.where(lane == j2 * 2 + p, jnp.sum(accs[p]), vec)
                        return vec

                    vec = lax.fori_loop(0, SC_LANES // 2, two_picks, jnp.zeros((SC_LANES,), F32))
                    a_v[t, pl.ds(q * SC_QUARTER + g * SC_LANES, SC_LANES)] = vec

        v_write(tpw - 1, nq - 1, (nq - 1) % 2).wait()
        pltpu.sync_copy(a_v, a_hbm.at[pl.ds(tok0, tpw)])

    return route_kernel(u_tbl, v_tbl, ids, h)


def _pack_rows(tbl):
    b = lax.bitcast_convert_type(tbl.astype(BF16), jnp.uint16).astype(U32)
    return b[:, :PACKED_WIDTH] | (b[:, PACKED_WIDTH:] << 16)


def _unpack(words):
    lo = lax.bitcast_convert_type(words << 16, F32)
    hi = lax.bitcast_convert_type(words & jnp.uint32(0xFFFF0000), F32)
    return lo, hi


def _lane_sum_replicated(x, ones_bf):
    hi = x.astype(BF16)
    lo = (x - hi.astype(F32)).astype(BF16)
    return (jnp.dot(hi, ones_bf, preferred_element_type=F32)
            + jnp.dot(lo, ones_bf, preferred_element_type=F32))


def _expert_body(a_ref, vg_ref, gate_ref, x1_ref, gf_ref, o_ref, y_ref, w_ref):
    tt = a_ref.shape[0]

    ones_bf = jnp.ones((128, 128), BF16)
    eye = (lax.broadcasted_iota(I32, (PEER_PICKS, PEER_PICKS), 0)
           == lax.broadcasted_iota(I32, (PEER_PICKS, PEER_PICKS), 1)).astype(F32)
    w_row = gate_ref[...] * jax.nn.gelu(a_ref[...])
    w_diag = (w_row[:, None, :] * eye[None]).reshape(tt * PEER_PICKS, PEER_PICKS)
    w_ref[...] = _lane_sum_replicated(w_diag, ones_bf)

    def weighted_rows(t, carry):
        r0 = pl.multiple_of(t * PEER_PICKS, PEER_PICKS)
        w = w_ref[pl.ds(r0, PEER_PICKS), :]
        vlo, vhi = _unpack(vg_ref[pl.ds(r0, PEER_PICKS), :])
        cols = [jnp.sum(w * half[:, j * 128:(j + 1) * 128], axis=0, keepdims=True)
                for half in (vlo, vhi) for j in range(4)]
        y_ref[pl.ds(t, 1), :] = jnp.concatenate(cols, axis=1)
        return carry

    lax.fori_loop(0, tt, weighted_rows, 0)
    o_ref[...] = _rms(x1_ref[...] + y_ref[...], gf_ref[...])


def _expert(a, vg, gates, x1, gf, first_token):
    n = a.shape[0]
    tt = TILE_EXPERT
    first = first_token // tt
    rows = pl.BlockSpec((tt * PEER_PICKS, PACKED_WIDTH), lambda i: (i, 0))
    tok = lambda w: pl.BlockSpec((tt, w), lambda i: (i, 0))
    resid = pl.BlockSpec((tt, D_MODEL), lambda i: (first + i, 0))
    return pl.pallas_call(
        _expert_body,
        grid=(n // tt,),
        in_specs=[tok(PEER_PICKS), rows, tok(PEER_PICKS), resid, _const_spec((1, D_MODEL))],
        out_specs=resid,
        out_shape=jax.ShapeDtypeStruct(x1.shape, F32),
        scratch_shapes=[pltpu.VMEM((tt, D_MODEL), F32), pltpu.VMEM((tt * PEER_PICKS, 128), F32)],
        input_output_aliases={3: 0},
        compiler_params=pltpu.CompilerParams(
            dimension_semantics=("parallel",), vmem_limit_bytes=VMEM_LIMIT),
        name="peer_expert",
    )(a, vg, gates, x1, gf)


def kernel(x, mem, positions, norm1_g, mem_norm_g, w_in, w_pool, pool_scale, attn_sinks, w_mem_kv, w_branch_pool, w_branch_swa, w_branch_mem, w_out, norm2_g, peer_w_q, peer_sub_keys, peer_u, peer_v, final_norm_g):
    batch, seq, d = x.shape
    assert norm1_g.shape[0] == 1, "single-layer trunk"
    half = SWA_HEAD_DIM // 2
    inv = ROPE_THETA ** (-jnp.arange(half, dtype=F32) / half)
    invf = jnp.tile(inv, 128 // half).reshape(1, 128)
    row = lambda v: v.reshape(1, -1)

    w_in_bf = w_in[0].astype(BF16)
    merge_weights = (
        w_pool[0].astype(BF16), row(pool_scale[0]), w_branch_pool[0].astype(BF16),
        w_branch_swa[0].astype(BF16), w_branch_mem[0].astype(BF16), w_out[0].astype(BF16),
        row(norm2_g[0]), peer_w_q[0].astype(BF16))
    sk = peer_sub_keys[0].reshape(2 * PEER_HEADS, N_KEYS, PEER_HALF).astype(BF16)
    u_pack = _pack_rows(peer_u[0])
    v_pack = _pack_rows(peer_v[0])
    gf = row(final_norm_g)
    km, vm = _memkv(mem, row(mem_norm_g[0]), w_mem_kv[0].astype(BF16))

    piece = seq // ROUTE_SPLITS
    outs = []
    for b in range(batch):
        xb = x[b]
        up, q, k, v, qm, gl = _inproj(xb, row(norm1_g[0]), w_in_bf)
        ys = _swa(q, k, v, positions[b].reshape(seq, 1), attn_sinks[0], invf, 1, seq)
        out, h2, qp = _merge(xb, up, ys, qm, gl, km[b:b + 1], vm[b:b + 1], *merge_weights, seq)
        for c in range(ROUTE_SPLITS):
            ids_t, gates_t = _topk(qp, sk, c * piece, piece)
            ids = ids_t.reshape(PEER_PICKS, piece).T
            gates = gates_t.reshape(PEER_PICKS, piece).T
            a, vg = _route_rows(u_pack, v_pack, ids, h2, c * piece)
            out = _expert(a, vg, gates, out, gf, c * piece)
        outs.append(out)
    return jnp.stack(outs, axis=0)
```

```python
import functools

import jax
import jax.numpy as jnp
from jax import lax
from jax.experimental import pallas as pl
from jax.experimental.pallas import tpu as pltpu
from jax.experimental.pallas import tpu_sc as plsc

F32 = jnp.float32
BF16 = jnp.bfloat16
I32 = jnp.int32
U32 = jnp.uint32

D_MODEL = 1024
EPS = 1e-6
NEG_INF = -1e30

POOL_WINDOWS = (2, 4, 8, 16)
POOL_GROUP_DIM = 128
POOL_WIDTH = 512
MAX_WINDOW = 16

SWA_HEAD_DIM = 64
SWA_HEADS = 16
SWA_WIDTH = 1024
SWA_KV_WIDTH = 128
SWA_BLOCK = 128
ROPE_THETA = 10000.0

MEM_HEADS = 4
MEM_HEAD_DIM = 128
MEM_WIDTH = 512
GATE_WIDTH = 3 * D_MODEL
IN_WIDTH = POOL_WIDTH + SWA_WIDTH + 2 * SWA_KV_WIDTH + MEM_WIDTH + GATE_WIDTH

PEER_HEADS = 8
N_KEYS = 128
PEER_HALF = 128
PEER_TOPK = 16
PEER_PICKS = PEER_HEADS * PEER_TOPK
PEER_Q_WIDTH = PEER_HEADS * 2 * PEER_HALF
PACKED_WIDTH = D_MODEL // 2

SC_CORES = 2
SC_SUBCORES = 16
SC_WORKERS = SC_CORES * SC_SUBCORES
SC_LANES = 16
SC_QUARTER = 32
SC_ACC_CHUNKS = 8
GELU_TANH_SCALE = 0.7978845608028654
GELU_TANH_CUBIC = 0.044715

VMEM_LIMIT = 48 * 1024 * 1024

TILE_INPROJ = 256
TILE_MERGE = 256
TILE_TOPK = 256
ROUTE_TAPER = 8


def _const_spec(shape):
    nd = len(shape)
    return pl.BlockSpec(shape, lambda *_: (0,) * nd, pipeline_mode=pl.Buffered(1))


def _rms(x, g):
    ms = jnp.mean(x * x, axis=-1, keepdims=True)
    return x * lax.rsqrt(ms + EPS) * g


def _dot_nt(a, b):
    return lax.dot_general(a, b, (((1,), (1,)), ((), ())), preferred_element_type=F32)


def _inproj_body(x_ref, g_ref, w_ref, up_ref, q_ref, k_ref, v_ref, qm_ref, gl_ref):
    h = _rms(x_ref[...], g_ref[...]).astype(BF16)

    def seg(a, b):
        return jnp.dot(h, w_ref[:, a:b], preferred_element_type=F32)

    o = 0
    up_ref[...] = seg(o, o + POOL_WIDTH)
    o += POOL_WIDTH
    q_ref[...] = seg(o, o + SWA_WIDTH)
    o += SWA_WIDTH
    k_ref[...] = seg(o, o + SWA_KV_WIDTH)
    o += SWA_KV_WIDTH
    v_ref[...] = seg(o, o + SWA_KV_WIDTH).astype(BF16)
    o += SWA_KV_WIDTH
    qm_ref[...] = seg(o, o + MEM_WIDTH).astype(BF16)
    o += MEM_WIDTH
    gl_ref[...] = seg(o, o + GATE_WIDTH)


def _inproj(x2, g, w_bf):
    t = x2.shape[0]
    tm = TILE_INPROJ
    row = lambda w: pl.BlockSpec((tm, w), lambda i: (i, 0))
    return pl.pallas_call(
        _inproj_body,
        grid=(t // tm,),
        in_specs=[row(D_MODEL), _const_spec((1, D_MODEL)), _const_spec((D_MODEL, IN_WIDTH))],
        out_specs=[row(POOL_WIDTH), row(SWA_WIDTH), row(SWA_KV_WIDTH), row(SWA_KV_WIDTH),
                   row(MEM_WIDTH), row(GATE_WIDTH)],
        out_shape=[
            jax.ShapeDtypeStruct((t, POOL_WIDTH), F32),
            jax.ShapeDtypeStruct((t, SWA_WIDTH), F32),
            jax.ShapeDtypeStruct((t, SWA_KV_WIDTH), F32),
            jax.ShapeDtypeStruct((t, SWA_KV_WIDTH), BF16),
            jax.ShapeDtypeStruct((t, MEM_WIDTH), BF16),
            jax.ShapeDtypeStruct((t, GATE_WIDTH), F32),
        ],
        compiler_params=pltpu.CompilerParams(
            dimension_semantics=("parallel",), vmem_limit_bytes=VMEM_LIMIT),
        name="inproj",
    )(x2, g, w_bf)


def _swa_body(sink_ref, q_ref, kc_ref, kp_ref, vc_ref, vp_ref, pc_ref, pp_ref, invf_ref, o_ref):
    n = pl.program_id(1)
    invf = invf_ref[...]
    lane = lax.broadcasted_iota(I32, (1, 2 * SWA_HEAD_DIM), 1)
    first_half = (lane % SWA_HEAD_DIM) < (SWA_HEAD_DIM // 2)
    lo_head = lane < SWA_HEAD_DIM

    def cos_sin(p_ref):
        ang = p_ref[...].astype(F32) * invf
        return jnp.cos(ang), jnp.sin(ang)

    def rope(x, cs):
        partner = jnp.where(first_half, -pltpu.roll(x, 96, 1), pltpu.roll(x, 32, 1))
        return x * cs[0] + partner * cs[1]

    cs_c = cos_sin(pc_ref)
    cs_p = cos_sin(pp_ref)
    k_all = jnp.concatenate([rope(kp_ref[...], cs_p), rope(kc_ref[...], cs_c)], axis=0)
    v_all = jnp.concatenate([vp_ref[...].astype(F32), vc_ref[...].astype(F32)], axis=0)

    def split(a):
        sw = pltpu.roll(a, SWA_HEAD_DIM, 1)
        zero = jnp.zeros_like(a)
        left = (jnp.where(lo_head, a, zero).astype(BF16), jnp.where(lo_head, sw, zero).astype(BF16))
        right = (jnp.where(lo_head, zero, sw).astype(BF16), jnp.where(lo_head, zero, a).astype(BF16))
        return left, right

    k_left, k_right = split(k_all)
    v_left, v_right = split(v_all)

    qi = lax.broadcasted_iota(I32, (SWA_BLOCK, 2 * SWA_BLOCK), 0)
    ki = lax.broadcasted_iota(I32, (SWA_BLOCK, 2 * SWA_BLOCK), 1)
    allowed = (ki > qi) & (ki <= qi + SWA_BLOCK) & ((ki >= SWA_BLOCK) | (n > 0))
    scale = SWA_HEAD_DIM ** -0.5

    for c in range(SWA_HEADS // 2):
        g = c // (SWA_HEADS // 4)
        qc = rope(q_ref[:, c * 128:(c + 1) * 128], cs_c).astype(BF16)
        acc = jnp.zeros((SWA_BLOCK, 128), F32)
        for par, (kx, vx) in enumerate(((k_left[g], v_left[g]), (k_right[g], v_right[g]))):
            sink = sink_ref[2 * c + par]
            s = _dot_nt(qc, kx) * scale
            s = jnp.where(allowed, s, NEG_INF)
            m = jnp.maximum(jnp.max(s, axis=-1, keepdims=True), sink)
            p = jnp.exp(s - m)
            den = jnp.sum(p, axis=-1, keepdims=True) + jnp.exp(sink - m)
            pn = (p / den).astype(BF16)
            acc = acc + jnp.dot(pn, vx, preferred_element_type=F32)
        o_ref[:, c * 128:(c + 1) * 128] = acc.astype(BF16)


def _swa(q, k, v, pos_col, sinks, invf, batch, seq):
    nb = seq // SWA_BLOCK
    t = batch * seq
    cur = lambda w: pl.BlockSpec((SWA_BLOCK, w), lambda b, n: (b * nb + n, 0))
    prev = lambda w: pl.BlockSpec((SWA_BLOCK, w), lambda b, n: (b * nb + jnp.maximum(n - 1, 0), 0))
    return pl.pallas_call(
        _swa_body,
        grid=(batch, nb),
        in_specs=[
            pl.BlockSpec(memory_space=pltpu.SMEM),
            cur(SWA_WIDTH), cur(SWA_KV_WIDTH), prev(SWA_KV_WIDTH),
            cur(SWA_KV_WIDTH), prev(SWA_KV_WIDTH), cur(1), prev(1),
            pl.BlockSpec((1, 128), lambda b, n: (0, 0)),
        ],
        out_specs=cur(SWA_WIDTH),
        out_shape=jax.ShapeDtypeStruct((t, SWA_WIDTH), BF16),
        compiler_params=pltpu.CompilerParams(
            dimension_semantics=("parallel", "parallel"), vmem_limit_bytes=VMEM_LIMIT),
        name="swa",
    )(sinks, q, k, k, v, v, pos_col, pos_col, invf)


def _memkv_body(mem_ref, g_ref, w_ref, km_ref, vm_ref):
    mn = _rms(mem_ref[0], g_ref[...]).astype(BF16)
    kv = jnp.dot(mn, w_ref[...], preferred_element_type=F32)
    km_ref[0] = kv[:, :MEM_WIDTH].astype(BF16)
    vm_ref[0] = kv[:, MEM_WIDTH:].astype(BF16)


def _memkv(mem, g, w_bf):
    b, m, _ = mem.shape
    blk = lambda w: pl.BlockSpec((1, m, w), lambda i: (i, 0, 0))
    return pl.pallas_call(
        _memkv_body,
        grid=(b,),
        in_specs=[blk(D_MODEL), _const_spec((1, D_MODEL)), _const_spec((D_MODEL, 2 * MEM_WIDTH))],
        out_specs=[blk(MEM_WIDTH), blk(MEM_WIDTH)],
        out_shape=[jax.ShapeDtypeStruct((b, m, MEM_WIDTH), BF16)] * 2,
        compiler_params=pltpu.CompilerParams(
            dimension_semantics=("parallel",), vmem_limit_bytes=VMEM_LIMIT),
        name="memkv",
    )(mem, g, w_bf)


def _merge_body(tiles_per_seq, x_ref, up_ref, halo_ref, ys_ref, qm_ref, gl_ref, km_ref, vm_ref,
                wpool_ref, pscale_ref, wbp_ref, wbs_ref, wbm_ref, wout_ref, g2_ref, wq_ref,
                x1_ref, h2_ref, qp_ref):
    tm = x_ref.shape[0]
    tile_in_seq = pl.program_id(0) % tiles_per_seq

    halo = jnp.where(tile_in_seq > 0, halo_ref[...], 0.0)
    t_in_seq = tile_in_seq * tm + lax.broadcasted_iota(I32, (tm, 1), 0)
    pooled = []
    for gi, w in enumerate(POOL_WINDOWS):
        cols = slice(gi * POOL_GROUP_DIM, (gi + 1) * POOL_GROUP_DIM)
        u = up_ref[:, cols]
        s = jnp.concatenate([halo[:, cols], u], axis=0)
        sh = 1
        while sh < w:
            s = s + pltpu.roll(s, sh, 0)
            sh *= 2
        cnt = jnp.minimum(t_in_seq + 1, w).astype(F32)
        pg = s[MAX_WINDOW:, :] / cnt - u
        mixed = jnp.dot(pg.astype(BF16), wpool_ref[gi], preferred_element_type=F32)
        pooled.append(mixed)
    y_pool = (jnp.concatenate(pooled, axis=1) * pscale_ref[...]).astype(BF16)

    mem_scale = MEM_HEAD_DIM ** -0.5
    y_mem = []
    for hh in range(MEM_HEADS):
        cols = slice(hh * MEM_HEAD_DIM, (hh + 1) * MEM_HEAD_DIM)
        s = _dot_nt(qm_ref[:, cols], km_ref[0, :, cols]) * mem_scale
        e = jnp.exp(s - jnp.max(s, axis=-1, keepdims=True))
        p = (e / jnp.sum(e, axis=-1, keepdims=True)).astype(BF16)
        y_mem.append(jnp.dot(p, vm_ref[0, :, cols], preferred_element_type=F32))
    y_mem = jnp.concatenate(y_mem, axis=1).astype(BF16)

    def gate(j):
        return jax.nn.sigmoid(gl_ref[:, j * D_MODEL:(j + 1) * D_MODEL])

    merged = gate(0) * jnp.dot(y_pool, wbp_ref[...], preferred_element_type=F32)
    merged = merged + gate(1) * jnp.dot(ys_ref[...], wbs_ref[...], preferred_element_type=F32)
    merged = merged + gate(2) * jnp.dot(y_mem, wbm_ref[...], preferred_element_type=F32)
    x1 = x_ref[...] + jnp.dot(merged.astype(BF16), wout_ref[...], preferred_element_type=F32)
    x1_ref[...] = x1

    h2 = _rms(x1, g2_ref[...]).astype(BF16)
    h2_ref[...] = h2.astype(F32)
    for j in range(2 * PEER_HEADS):
        qp_ref[j] = jnp.dot(h2, wq_ref[:, j * PEER_HALF:(j + 1) * PEER_HALF],
                            preferred_element_type=F32).astype(BF16)


def _merge(x2, up, ys, qm, gl, km, vm, wpool, pscale, wbp, wbs, wbm, wout, g2, wq, seq):
    t = x2.shape[0]
    tm = TILE_MERGE
    tiles_per_seq = seq // tm
    halo_blocks = tm // MAX_WINDOW
    mem_len = km.shape[1]
    row = lambda w: pl.BlockSpec((tm, w), lambda i: (i, 0))
    memblk = pl.BlockSpec((1, mem_len, MEM_WIDTH), lambda i: (i // tiles_per_seq, 0, 0))
    return pl.pallas_call(
        functools.partial(_merge_body, tiles_per_seq),
        grid=(t // tm,),
        in_specs=[
            row(D_MODEL), row(POOL_WIDTH),
            pl.BlockSpec((MAX_WINDOW, POOL_WIDTH), lambda i: (jnp.maximum(i * halo_blocks - 1, 0), 0)),
            row(SWA_WIDTH), row(MEM_WIDTH), row(GATE_WIDTH), memblk, memblk,
            _const_spec((len(POOL_WINDOWS), POOL_GROUP_DIM, POOL_GROUP_DIM)),
            _const_spec((1, POOL_WIDTH)),
            _const_spec((POOL_WIDTH, D_MODEL)), _const_spec((SWA_WIDTH, D_MODEL)),
            _const_spec((MEM_WIDTH, D_MODEL)), _const_spec((D_MODEL, D_MODEL)),
            _const_spec((1, D_MODEL)), _const_spec((D_MODEL, PEER_Q_WIDTH)),
        ],
        out_specs=[row(D_MODEL), row(D_MODEL),
                   pl.BlockSpec((2 * PEER_HEADS, tm, PEER_HALF), lambda i: (0, i, 0))],
        out_shape=[
            jax.ShapeDtypeStruct((t, D_MODEL), F32),
            jax.ShapeDtypeStruct((t, D_MODEL), F32),
            jax.ShapeDtypeStruct((2 * PEER_HEADS, t, PEER_HALF), BF16),
        ],
        compiler_params=pltpu.CompilerParams(
            dimension_semantics=("parallel",), vmem_limit_bytes=VMEM_LIMIT),
        name="merge",
    )(x2, up, up, ys, qm, gl, km, vm, wpool, pscale, wbp, wbs, wbm, wout, g2, wq)


def _topk_body(qp_ref, sk_ref, ids_ref, gates_ref):
    tt = qp_ref.shape[1]
    k = PEER_TOPK
    row_iota = lax.broadcasted_iota(I32, (k, tt), 0)
    sub_iota = lax.broadcasted_iota(I32, (8, tt), 0)

    def top16(sc, payload=None):
        n_rows = sc.shape[0]
        pos = lax.broadcasted_iota(I32, (n_rows, tt), 0).astype(F32)
        vals = jnp.zeros((k, tt), F32)
        picked = jnp.zeros((k, tt), F32)
        for i in range(k):
            m = jnp.max(sc, axis=0, keepdims=True)
            r = jnp.min(jnp.where(sc == m, pos, float(n_rows)), axis=0, keepdims=True)
            hit = pos == r
            out = r if payload is None else jnp.max(jnp.where(hit, payload, -1.0), axis=0, keepdims=True)
            vals = jnp.where(row_iota == i, m, vals)
            picked = jnp.where(row_iota == i, out, picked)
            sc = jnp.where(hit, -jnp.inf, sc)
        return vals, picked

    s0, i0 = top16(_dot_nt(sk_ref[0], qp_ref[0]))
    s1, i1 = top16(_dot_nt(sk_ref[1], qp_ref[1]))

    cand, cid = [], []
    for a in range(k // 2):
        n_valid = k // (a + 1)
        for piece in range((n_valid + 7) // 8):
            rows = slice(piece * 8, piece * 8 + 8)
            sums = s0[a:a + 1, :] + s1[rows, :]
            if n_valid - piece * 8 < 8:
                sums = jnp.where(sub_iota < n_valid - piece * 8, sums, -jnp.inf)
            cand.append(sums)
            cid.append(i0[a:a + 1, :] * float(N_KEYS) + i1[rows, :])
    cand.append(s0[k // 2:, :] + s1[0:1, :])
    cid.append(i0[k // 2:, :] * float(N_KEYS) + i1[0:1, :])
    s_fin, ids = top16(jnp.concatenate(cand, axis=0), payload=jnp.concatenate(cid, axis=0))

    e = jnp.exp(s_fin - s_fin[0:1, :])
    gates_ref[0] = e / jnp.sum(e, axis=0, keepdims=True)
    ids_ref[0] = ids.astype(I32)


def _topk(qp, sk_bf, first_token, n_tokens):
    tt = TILE_TOPK
    first = first_token // tt
    out_blk = pl.BlockSpec((1, PEER_TOPK, tt), lambda i, h: (h, 0, i))
    return pl.pallas_call(
        _topk_body,
        grid=(n_tokens // tt, PEER_HEADS),
        in_specs=[
            pl.BlockSpec((2, tt, PEER_HALF), lambda i, h: (h, first + i, 0)),
            pl.BlockSpec((2, N_KEYS, PEER_HALF), lambda i, h: (h, 0, 0)),
        ],
        out_specs=[out_blk, out_blk],
        out_shape=[
            jax.ShapeDtypeStruct((PEER_HEADS, PEER_TOPK, n_tokens), I32),
            jax.ShapeDtypeStruct((PEER_HEADS, PEER_TOPK, n_tokens), F32),
        ],
        compiler_params=pltpu.CompilerParams(
            dimension_semantics=("parallel", "parallel"), vmem_limit_bytes=VMEM_LIMIT),
        name="peer_topk",
    )(qp, sk_bf)


def _route_all(u_tbl, v_tbl, ids, gates, h, h_first):
    s = ids.shape[0]
    tpw = s // SC_WORKERS
    assert tpw * SC_WORKERS == s and tpw >= 2 and ids.shape[1] == PEER_PICKS
    nq = PEER_PICKS // SC_QUARTER
    n_chunks = PACKED_WIDTH // SC_LANES
    mesh = plsc.VectorSubcoreMesh(core_axis_name="core", subcore_axis_name="subcore")

    @functools.partial(
        pl.kernel,
        out_type=jax.ShapeDtypeStruct((s, D_MODEL), F32),
        mesh=mesh,
        scratch_types=[
            pltpu.VMEM((tpw, PEER_PICKS), I32),
            pltpu.VMEM((tpw, PEER_PICKS), F32),
            pltpu.VMEM((2, SC_QUARTER, PACKED_WIDTH), U32),
            pltpu.VMEM((2, SC_QUARTER, PACKED_WIDTH), U32),
            pltpu.VMEM((2, D_MODEL), F32),
            pltpu.VMEM((2, D_MODEL), F32),
            pltpu.VMEM((PEER_PICKS,), F32),
            pltpu.VMEM((PEER_PICKS, SC_LANES), F32),
            pltpu.SemaphoreType.DMA((2,)),
            pltpu.SemaphoreType.DMA((2,)),
            pltpu.SemaphoreType.DMA((2,)),
            pltpu.SemaphoreType.DMA((2,)),
        ],
        compiler_params=pltpu.CompilerParams(needs_layout_passes=False),
    )
    def route_kernel(u_hbm, v_hbm, idx_hbm, g_hbm, h_hbm, y_hbm,
                     idx_v, gw_v, u_v, v_v, h_v, y_v, a_v, wb_v, u_sem, v_sem, h_sem, y_sem):
        worker = lax.axis_index("subcore") * SC_CORES + lax.axis_index("core")
        tok0 = worker * tpw
        pltpu.sync_copy(idx_hbm.at[pl.ds(tok0, tpw)], idx_v)
        pltpu.sync_copy(g_hbm.at[pl.ds(tok0, tpw)], gw_v)
        lane = lax.iota(I32, SC_LANES)

        def u_gather(t, q, b):
            idx = idx_v.at[t, pl.ds(q * SC_QUARTER, SC_QUARTER)]
            return pltpu.make_async_copy(u_hbm.at[idx], u_v.at[b], u_sem.at[b])

        def v_gather(t, q, b):
            idx = idx_v.at[t, pl.ds(q * SC_QUARTER, SC_QUARTER)]
            return pltpu.make_async_copy(v_hbm.at[idx], v_v.at[b], v_sem.at[b])

        def h_load(t, hb):
            return pltpu.make_async_copy(h_hbm.at[h_first + tok0 + t], h_v.at[hb], h_sem.at[hb])

        def y_store(t, yb):
            return pltpu.make_async_copy(y_v.at[yb], y_hbm.at[tok0 + t], y_sem.at[yb])

        def unpack(words):
            return (lax.bitcast_convert_type(words << 16, F32),
                    lax.bitcast_convert_type(words & jnp.uint32(0xFFFF0000), F32))

        h_load(0, 0).start()
        u_gather(0, 0, 0).start()
        v_gather(0, 0, 0).start()

        @pl.loop(0, tpw + 1)
        def _(t):
            hb = t % 2
            tv = t - 1
            yb = (t + 1) % 2
            do_u = t < tpw
            do_v = t >= 1

            @pl.when(do_u)
            def _():
                h_load(t, hb).wait()

                @pl.when(t + 1 < tpw)
                def _():
                    h_load(t + 1, 1 - hb).start()

            @pl.when(tv >= 2)
            def _():
                y_store(tv - 2, yb).wait()

            @pl.loop(0, nq)
            def _(q):
                b = q % 2
                last_q = q == nq - 1
                qn = jnp.where(last_q, 0, q + 1)

                @pl.when(do_u)
                def _():
                    u_gather(t, q, b).wait()
                    tn = jnp.where(last_q, t + 1, t)

                    @pl.when(tn < tpw)
                    def _():
                        u_gather(tn, qn, 1 - b).start()

                    for g in range(SC_QUARTER // SC_LANES):
                        def two_picks(j2, vec):
                            k0 = g * SC_LANES + j2 * 2
                            accs = [jnp.zeros((SC_LANES,), F32) for _ in range(2)]
                            for c in range(n_chunks):
                                h_lo = h_v[hb, pl.ds(SC_LANES * c, SC_LANES)]
                                h_hi = h_v[hb, pl.ds(PACKED_WIDTH + SC_LANES * c, SC_LANES)]
                                for p in range(2):
                                    lo, hi = unpack(u_v[b, k0 + p, pl.ds(SC_LANES * c, SC_LANES)])
                                    accs[p] = accs[p] + lo * h_lo + hi * h_hi
                            for p in range(2):
                                vec = jnp.where(lane == j2 * 2 + p, jnp.sum(accs[p]), vec)
                            return vec

                        vec = lax.fori_loop(0, SC_LANES // 2, two_picks, jnp.zeros((SC_LANES,), F32))
                        a_v[pl.ds(q * SC_QUARTER + g * SC_LANES, SC_LANES)] = vec

                @pl.when(do_v)
                def _():
                    v_gather(tv, q, b).wait()
                    tvn = jnp.where(last_q, tv + 1, tv)

                    @pl.when(tvn < tpw)
                    def _():
                        v_gather(tvn, qn, 1 - b).start()

                    first_q = q == 0
                    for fg in range(n_chunks // SC_ACC_CHUNKS):
                        lo_at = [pl.ds(SC_LANES * (fg * SC_ACC_CHUNKS + c), SC_LANES)
                                 for c in range(SC_ACC_CHUNKS)]
                        hi_at = [pl.ds(PACKED_WIDTH + SC_LANES * (fg * SC_ACC_CHUNKS + c), SC_LANES)
                                 for c in range(SC_ACC_CHUNKS)]
                        zero = jnp.zeros((SC_LANES,), F32)
                        init = tuple([jnp.where(first_q, zero, y_v[yb, sl]) for sl in lo_at]
                                     + [jnp.where(first_q, zero, y_v[yb, sl]) for sl in hi_at])

                        def two_rows(k2, accs, lo_at=lo_at):
                            accs = list(accs)
                            for p in range(2):
                                kk = k2 * 2 + p
                                wk = wb_v[q * SC_QUARTER + kk]
                                for c in range(SC_ACC_CHUNKS):
                                    lo, hi = unpack(v_v[b, kk, lo_at[c]])
                                    accs[c] = accs[c] + wk * lo
                                    accs[SC_ACC_CHUNKS + c] = accs[SC_ACC_CHUNKS + c] + wk * hi
                            return tuple(accs)

                        accs = lax.fori_loop(0, SC_QUARTER // 2, two_rows, init)
                        for c in range(SC_ACC_CHUNKS):
                            y_v[yb, lo_at[c]] = accs[c]
                            y_v[yb, hi_at[c]] = accs[SC_ACC_CHUNKS + c]

            @pl.when(do_v)
            def _():
                y_store(tv, yb).start()

            @pl.when(do_u)
            def _():
                for g in range(PEER_PICKS // SC_LANES):
                    sl = pl.ds(g * SC_LANES, SC_LANES)
                    a = a_v[sl]
                    z = GELU_TANH_SCALE * (a + GELU_TANH_CUBIC * a * a * a)
                    tanh_z = 1.0 - 2.0 / (jnp.exp(2.0 * z) + 1.0)
                    w = gw_v[t, sl] * (0.5 * a * (1.0 + tanh_z))
                    for j in range(SC_LANES):
                        lane_j = jnp.full((SC_LANES,), j, I32)
                        wb_v[g * SC_LANES + j] = w.at[lane_j].get(mode="promise_in_bounds")

        y_store(tpw - 1, (tpw - 1) % 2).wait()
        y_store(tpw - 2, (tpw - 2) % 2).wait()

    return route_kernel(u_tbl, v_tbl, ids, gates, h)


def _finish_body(x1_ref, y_ref, gf_ref, o_ref):
    o_ref[...] = _rms(x1_ref[...] + y_ref[...], gf_ref[...])


def _finish(y, x1, gf, first_token):
    n = y.shape[0]
    tt = TILE_MERGE
    first = first_token // tt
    resid = pl.BlockSpec((tt, D_MODEL), lambda i: (first + i, 0))
    return pl.pallas_call(
        _finish_body,
        grid=(n // tt,),
        in_specs=[resid, pl.BlockSpec((tt, D_MODEL), lambda i: (i, 0)), _const_spec((1, D_MODEL))],
        out_specs=resid,
        out_shape=jax.ShapeDtypeStruct(x1.shape, F32),
        input_output_aliases={0: 0},
        compiler_params=pltpu.CompilerParams(
            dimension_semantics=("parallel",), vmem_limit_bytes=VMEM_LIMIT),
        name="peer_finish",
    )(x1, y, gf)


def _pack_rows(tbl):
    b = lax.bitcast_convert_type(tbl.astype(BF16), jnp.uint16).astype(U32)
    return b[:, :PACKED_WIDTH] | (b[:, PACKED_WIDTH:] << 16)


def _route_pieces(b, batch, seq):
    half, small = seq // 2, seq // ROUTE_TAPER
    taper = small % max(TILE_TOPK, SC_WORKERS * 2) == 0 and small < half
    head = [small, half - small] if (taper and b == batch - 1) else [half]
    tail = [half - small, small] if (taper and b == 0) else [half]
    return head + tail


def kernel(x, mem, positions, norm1_g, mem_norm_g, w_in, w_pool, pool_scale, attn_sinks, w_mem_kv, w_branch_pool, w_branch_swa, w_branch_mem, w_out, norm2_g, peer_w_q, peer_sub_keys, peer_u, peer_v, final_norm_g):
    batch, seq, d = x.shape
    assert norm1_g.shape[0] == 1, "single-layer trunk"
    half = SWA_HEAD_DIM // 2
    inv = ROPE_THETA ** (-jnp.arange(half, dtype=F32) / half)
    invf = jnp.tile(inv, 128 // half).reshape(1, 128)
    row = lambda v: v.reshape(1, -1)

    w_in_bf = w_in[0].astype(BF16)
    merge_weights = (
        w_pool[0].astype(BF16), row(pool_scale[0]), w_branch_pool[0].astype(BF16),
        w_branch_swa[0].astype(BF16), w_branch_mem[0].astype(BF16), w_out[0].astype(BF16),
        row(norm2_g[0]), peer_w_q[0].astype(BF16))
    sk = peer_sub_keys[0].reshape(2 * PEER_HEADS, N_KEYS, PEER_HALF).astype(BF16)
    u_pack = _pack_rows(peer_u[0])
    v_pack = _pack_rows(peer_v[0])
    gf = row(final_norm_g)
    km, vm = _memkv(mem, row(mem_norm_g[0]), w_mem_kv[0].astype(BF16))

    outs = []
    for b in range(batch):
        xb = x[b]
        up, q, k, v, qm, gl = _inproj(xb, row(norm1_g[0]), w_in_bf)
        ys = _swa(q, k, v, positions[b].reshape(seq, 1), attn_sinks[0], invf, 1, seq)
        out, h2, qp = _merge(xb, up, ys, qm, gl, km[b:b + 1], vm[b:b + 1], *merge_weights, seq)
        first = 0
        for piece in _route_pieces(b, batch, seq):
            ids_t, gates_t = _topk(qp, sk, first, piece)
            ids = ids_t.reshape(PEER_PICKS, piece).T
            gates = gates_t.reshape(PEER_PICKS, piece).T
            y = _route_all(u_pack, v_pack, ids, gates, h2, first)
            out = _finish(y, out, gf, first)
            first += piece
        outs.append(out)
    return jnp.stack(outs, axis=0)
```

```python
import functools

import jax
import jax.numpy as jnp
from jax import lax
from jax.experimental import pallas as pl
from jax.experimental.pallas import tpu as pltpu
from jax.experimental.pallas import tpu_sc as plsc

F32 = jnp.float32
BF16 = jnp.bfloat16
I32 = jnp.int32
U32 = jnp.uint32

D_MODEL = 1024
EPS = 1e-6
NEG_INF = -1e30

POOL_WINDOWS = (2, 4, 8, 16)
POOL_GROUP_DIM = 128
POOL_WIDTH = 512
MAX_WINDOW = 16

SWA_HEAD_DIM = 64
SWA_HEADS = 16
SWA_WIDTH = 1024
SWA_KV_WIDTH = 128
SWA_BLOCK = 128
ROPE_THETA = 10000.0

MEM_HEADS = 4
MEM_HEAD_DIM = 128
MEM_WIDTH = 512
GATE_WIDTH = 3 * D_MODEL
IN_WIDTH = POOL_WIDTH + SWA_WIDTH + 2 * SWA_KV_WIDTH + MEM_WIDTH + GATE_WIDTH

PEER_HEADS = 8
N_KEYS = 128
PEER_HALF = 128
PEER_TOPK = 16
PEER_PICKS = PEER_HEADS * PEER_TOPK
PEER_Q_WIDTH = PEER_HEADS * 2 * PEER_HALF
PACKED_WIDTH = D_MODEL // 2

SC_CORES = 2
SC_SUBCORES = 16
SC_WORKERS = SC_CORES * SC_SUBCORES
SC_LANES = 16
SC_QUARTER = 32
SC_ACC_CHUNKS = 8
GELU_TANH_SCALE = 0.7978845608028654
GELU_TANH_CUBIC = 0.044715

VMEM_LIMIT = 48 * 1024 * 1024

TILE_INPROJ = 256
TILE_MERGE = 256
TILE_TOPK = 256
TILE_PACK = 1024
ROUTE_TAPER = 8


def _const_spec(shape):
    nd = len(shape)
    return pl.BlockSpec(shape, lambda *_: (0,) * nd, pipeline_mode=pl.Buffered(1))


def _rms(x, g):
    ms = jnp.mean(x * x, axis=-1, keepdims=True)
    return x * lax.rsqrt(ms + EPS) * g


def _dot_nt(a, b):
    return lax.dot_general(a, b, (((1,), (1,)), ((), ())), preferred_element_type=F32)


def _inproj_body(x_ref, g_ref, w_ref, up_ref, q_ref, k_ref, v_ref, qm_ref, gl_ref):
    h = _rms(x_ref[...], g_ref[...]).astype(BF16)

    def seg(a, b):
        return jnp.dot(h, w_ref[:, a:b], preferred_element_type=F32)

    o = 0
    up_ref[...] = seg(o, o + POOL_WIDTH)
    o += POOL_WIDTH
    q_ref[...] = seg(o, o + SWA_WIDTH)
    o += SWA_WIDTH
    k_ref[...] = seg(o, o + SWA_KV_WIDTH)
    o += SWA_KV_WIDTH
    v_ref[...] = seg(o, o + SWA_KV_WIDTH).astype(BF16)
    o += SWA_KV_WIDTH
    qm_ref[...] = seg(o, o + MEM_WIDTH).astype(BF16)
    o += MEM_WIDTH
    gl_ref[...] = seg(o, o + GATE_WIDTH)


def _inproj(x2, g, w_bf):
    t = x2.shape[0]
    tm = TILE_INPROJ
    row = lambda w: pl.BlockSpec((tm, w), lambda i: (i, 0))
    return pl.pallas_call(
        _inproj_body,
        grid=(t // tm,),
        in_specs=[row(D_MODEL), _const_spec((1, D_MODEL)), _const_spec((D_MODEL, IN_WIDTH))],
        out_specs=[row(POOL_WIDTH), row(SWA_WIDTH), row(SWA_KV_WIDTH), row(SWA_KV_WIDTH),
                   row(MEM_WIDTH), row(GATE_WIDTH)],
        out_shape=[
            jax.ShapeDtypeStruct((t, POOL_WIDTH), F32),
            jax.ShapeDtypeStruct((t, SWA_WIDTH), F32),
            jax.ShapeDtypeStruct((t, SWA_KV_WIDTH), F32),
            jax.ShapeDtypeStruct((t, SWA_KV_WIDTH), BF16),
            jax.ShapeDtypeStruct((t, MEM_WIDTH), BF16),
            jax.ShapeDtypeStruct((t, GATE_WIDTH), F32),
        ],
        compiler_params=pltpu.CompilerParams(
            dimension_semantics=("parallel",), vmem_limit_bytes=VMEM_LIMIT),
        name="inproj",
    )(x2, g, w_bf)


def _swa_body(sink_ref, q_ref, kc_ref, kp_ref, vc_ref, vp_ref, pc_ref, pp_ref, invf_ref, o_ref):
    n = pl.program_id(1)
    invf = invf_ref[...]
    lane = lax.broadcasted_iota(I32, (1, 2 * SWA_HEAD_DIM), 1)
    first_half = (lane % SWA_HEAD_DIM) < (SWA_HEAD_DIM // 2)
    lo_head = lane < SWA_HEAD_DIM

    def cos_sin(p_ref):
        ang = p_ref[...].astype(F32) * invf
        return jnp.cos(ang), jnp.sin(ang)

    def rope(x, cs):
        partner = jnp.where(first_half, -pltpu.roll(x, 96, 1), pltpu.roll(x, 32, 1))
        return x * cs[0] + partner * cs[1]

    cs_c = cos_sin(pc_ref)
    cs_p = cos_sin(pp_ref)
    k_all = jnp.concatenate([rope(kp_ref[...], cs_p), rope(kc_ref[...], cs_c)], axis=0)
    v_all = jnp.concatenate([vp_ref[...].astype(F32), vc_ref[...].astype(F32)], axis=0)

    def split(a):
        sw = pltpu.roll(a, SWA_HEAD_DIM, 1)
        zero = jnp.zeros_like(a)
        left = (jnp.where(lo_head, a, zero).astype(BF16), jnp.where(lo_head, sw, zero).astype(BF16))
        right = (jnp.where(lo_head, zero, sw).astype(BF16), jnp.where(lo_head, zero, a).astype(BF16))
        return left, right

    k_left, k_right = split(k_all)
    v_left, v_right = split(v_all)

    qi = lax.broadcasted_iota(I32, (SWA_BLOCK, 2 * SWA_BLOCK), 0)
    ki = lax.broadcasted_iota(I32, (SWA_BLOCK, 2 * SWA_BLOCK), 1)
    allowed = (ki > qi) & (ki <= qi + SWA_BLOCK) & ((ki >= SWA_BLOCK) | (n > 0))
    scale = SWA_HEAD_DIM ** -0.5

    for c in range(SWA_HEADS // 2):
        g = c // (SWA_HEADS // 4)
        qc = rope(q_ref[:, c * 128:(c + 1) * 128], cs_c).astype(BF16)
        acc = jnp.zeros((SWA_BLOCK, 128), F32)
        for par, (kx, vx) in enumerate(((k_left[g], v_left[g]), (k_right[g], v_right[g]))):
            sink = sink_ref[2 * c + par]
            s = _dot_nt(qc, kx) * scale
            s = jnp.where(allowed, s, NEG_INF)
            m = jnp.maximum(jnp.max(s, axis=-1, keepdims=True), sink)
            p = jnp.exp(s - m)
            den = jnp.sum(p, axis=-1, keepdims=True) + jnp.exp(sink - m)
            pn = (p / den).astype(BF16)
            acc = acc + jnp.dot(pn, vx, preferred_element_type=F32)
        o_ref[:, c * 128:(c + 1) * 128] = acc.astype(BF16)


def _swa(q, k, v, pos_col, sinks, invf, batch, seq):
    nb = seq // SWA_BLOCK
    t = batch * seq
    cur = lambda w: pl.BlockSpec((SWA_BLOCK, w), lambda b, n: (b * nb + n, 0))
    prev = lambda w: pl.BlockSpec((SWA_BLOCK, w), lambda b, n: (b * nb + jnp.maximum(n - 1, 0), 0))
    return pl.pallas_call(
        _swa_body,
        grid=(batch, nb),
        in_specs=[
            pl.BlockSpec(memory_space=pltpu.SMEM),
            cur(SWA_WIDTH), cur(SWA_KV_WIDTH), prev(SWA_KV_WIDTH),
            cur(SWA_KV_WIDTH), prev(SWA_KV_WIDTH), cur(1), prev(1),
            pl.BlockSpec((1, 128), lambda b, n: (0, 0)),
        ],
        out_specs=cur(SWA_WIDTH),
        out_shape=jax.ShapeDtypeStruct((t, SWA_WIDTH), BF16),
        compiler_params=pltpu.CompilerParams(
            dimension_semantics=("parallel", "parallel"), vmem_limit_bytes=VMEM_LIMIT),
        name="swa",
    )(sinks, q, k, k, v, v, pos_col, pos_col, invf)


def _memkv_body(mem_ref, g_ref, w_ref, km_ref, vm_ref):
    mn = _rms(mem_ref[0], g_ref[...]).astype(BF16)
    kv = jnp.dot(mn, w_ref[...], preferred_element_type=F32)
    km_ref[0] = kv[:, :MEM_WIDTH].astype(BF16)
    vm_ref[0] = kv[:, MEM_WIDTH:].astype(BF16)


def _memkv(mem, g, w_bf):
    b, m, _ = mem.shape
    blk = lambda w: pl.BlockSpec((1, m, w), lambda i: (i, 0, 0))
    return pl.pallas_call(
        _memkv_body,
        grid=(b,),
        in_specs=[blk(D_MODEL), _const_spec((1, D_MODEL)), _const_spec((D_MODEL, 2 * MEM_WIDTH))],
        out_specs=[blk(MEM_WIDTH), blk(MEM_WIDTH)],
        out_shape=[jax.ShapeDtypeStruct((b, m, MEM_WIDTH), BF16)] * 2,
        compiler_params=pltpu.CompilerParams(
            dimension_semantics=("parallel",), vmem_limit_bytes=VMEM_LIMIT),
        name="memkv",
    )(mem, g, w_bf)


def _merge_body(tiles_per_seq, x_ref, up_ref, halo_ref, ys_ref, qm_ref, gl_ref, km_ref, vm_ref,
                wpool_ref, pscale_ref, wbp_ref, wbs_ref, wbm_ref, wout_ref, g2_ref, wq_ref,
                x1_ref, h2_ref, qp_ref):
    tm = x_ref.shape[0]
    tile_in_seq = pl.program_id(0) % tiles_per_seq

    halo = jnp.where(tile_in_seq > 0, halo_ref[...], 0.0)
    t_in_seq = tile_in_seq * tm + lax.broadcasted_iota(I32, (tm, 1), 0)
    pooled = []
    for gi, w in enumerate(POOL_WINDOWS):
        cols = slice(gi * POOL_GROUP_DIM, (gi + 1) * POOL_GROUP_DIM)
        u = up_ref[:, cols]
        s = jnp.concatenate([halo[:, cols], u], axis=0)
        sh = 1
        while sh < w:
            s = s + pltpu.roll(s, sh, 0)
            sh *= 2
        cnt = jnp.minimum(t_in_seq + 1, w).astype(F32)
        pg = s[MAX_WINDOW:, :] / cnt - u
        mixed = jnp.dot(pg.astype(BF16), wpool_ref[gi], preferred_element_type=F32)
        pooled.append(mixed)
    y_pool = (jnp.concatenate(pooled, axis=1) * pscale_ref[...]).astype(BF16)

    mem_scale = MEM_HEAD_DIM ** -0.5
    y_mem = []
    for hh in range(MEM_HEADS):
        cols = slice(hh * MEM_HEAD_DIM, (hh + 1) * MEM_HEAD_DIM)
        s = _dot_nt(qm_ref[:, cols], km_ref[0, :, cols]) * mem_scale
        e = jnp.exp(s - jnp.max(s, axis=-1, keepdims=True))
        p = (e / jnp.sum(e, axis=-1, keepdims=True)).astype(BF16)
        y_mem.append(jnp.dot(p, vm_ref[0, :, cols], preferred_element_type=F32))
    y_mem = jnp.concatenate(y_mem, axis=1).astype(BF16)

    def gate(j):
        return jax.nn.sigmoid(gl_ref[:, j * D_MODEL:(j + 1) * D_MODEL])

    merged = gate(0) * jnp.dot(y_pool, wbp_ref[...], preferred_element_type=F32)
    merged = merged + gate(1) * jnp.dot(ys_ref[...], wbs_ref[...], preferred_element_type=F32)
    merged = merged + gate(2) * jnp.dot(y_mem, wbm_ref[...], preferred_element_type=F32)
    x1 = x_ref[...] + jnp.dot(merged.astype(BF16), wout_ref[...], preferred_element_type=F32)
    x1_ref[...] = x1

    h2 = _rms(x1, g2_ref[...]).astype(BF16)
    h2_ref[...] = h2.astype(F32)
    for j in range(2 * PEER_HEADS):
        qp_ref[j] = jnp.dot(h2, wq_ref[:, j * PEER_HALF:(j + 1) * PEER_HALF],
                            preferred_element_type=F32).astype(BF16)


def _merge(x2, up, ys, qm, gl, km, vm, wpool, pscale, wbp, wbs, wbm, wout, g2, wq, seq):
    t = x2.shape[0]
    tm = TILE_MERGE
    tiles_per_seq = seq // tm
    halo_blocks = tm // MAX_WINDOW
    mem_len = km.shape[1]
    row = lambda w: pl.BlockSpec((tm, w), lambda i: (i, 0))
    memblk = pl.BlockSpec((1, mem_len, MEM_WIDTH), lambda i: (i // tiles_per_seq, 0, 0))
    return pl.pallas_call(
        functools.partial(_merge_body, tiles_per_seq),
        grid=(t // tm,),
        in_specs=[
            row(D_MODEL), row(POOL_WIDTH),
            pl.BlockSpec((MAX_WINDOW, POOL_WIDTH), lambda i: (jnp.maximum(i * halo_blocks - 1, 0), 0)),
            row(SWA_WIDTH), row(MEM_WIDTH), row(GATE_WIDTH), memblk, memblk,
            _const_spec((len(POOL_WINDOWS), POOL_GROUP_DIM, POOL_GROUP_DIM)),
            _const_spec((1, POOL_WIDTH)),
            _const_spec((POOL_WIDTH, D_MODEL)), _const_spec((SWA_WIDTH, D_MODEL)),
            _const_spec((MEM_WIDTH, D_MODEL)), _const_spec((D_MODEL, D_MODEL)),
            _const_spec((1, D_MODEL)), _const_spec((D_MODEL, PEER_Q_WIDTH)),
        ],
        out_specs=[row(D_MODEL), row(D_MODEL),
                   pl.BlockSpec((2 * PEER_HEADS, tm, PEER_HALF), lambda i: (0, i, 0))],
        out_shape=[
            jax.ShapeDtypeStruct((t, D_MODEL), F32),
            jax.ShapeDtypeStruct((t, D_MODEL), F32),
            jax.ShapeDtypeStruct((2 * PEER_HEADS, t, PEER_HALF), BF16),
        ],
        compiler_params=pltpu.CompilerParams(
            dimension_semantics=("parallel",), vmem_limit_bytes=VMEM_LIMIT),
        name="merge",
    )(x2, up, up, ys, qm, gl, km, vm, wpool, pscale, wbp, wbs, wbm, wout, g2, wq)


def _topk_body(qp_ref, sk_ref, ids_ref, gates_ref):
    tt = qp_ref.shape[1]
    k = PEER_TOPK
    row_iota = lax.broadcasted_iota(I32, (k, tt), 0)
    sub_iota = lax.broadcasted_iota(I32, (8, tt), 0)

    def top16(sc, payload=None):
        n_rows = sc.shape[0]
        pos = lax.broadcasted_iota(I32, (n_rows, tt), 0).astype(F32)
        vals = jnp.zeros((k, tt), F32)
        picked = jnp.zeros((k, tt), F32)
        for i in range(k):
            m = jnp.max(sc, axis=0, keepdims=True)
            r = jnp.min(jnp.where(sc == m, pos, float(n_rows)), axis=0, keepdims=True)
            hit = pos == r
            out = r if payload is None else jnp.max(jnp.where(hit, payload, -1.0), axis=0, keepdims=True)
            vals = jnp.where(row_iota == i, m, vals)
            picked = jnp.where(row_iota == i, out, picked)
            sc = jnp.where(hit, -jnp.inf, sc)
        return vals, picked

    s0, i0 = top16(_dot_nt(sk_ref[0], qp_ref[0]))
    s1, i1 = top16(_dot_nt(sk_ref[1], qp_ref[1]))

    cand, cid = [], []
    for a in range(k // 2):
        n_valid = k // (a + 1)
        for piece in range((n_valid + 7) // 8):
            rows = slice(piece * 8, piece * 8 + 8)
            sums = s0[a:a + 1, :] + s1[rows, :]
            if n_valid - piece * 8 < 8:
                sums = jnp.where(sub_iota < n_valid - piece * 8, sums, -jnp.inf)
            cand.append(sums)
            cid.append(i0[a:a + 1, :] * float(N_KEYS) + i1[rows, :])
    cand.append(s0[k // 2:, :] + s1[0:1, :])
    cid.append(i0[k // 2:, :] * float(N_KEYS) + i1[0:1, :])
    s_fin, ids = top16(jnp.concatenate(cand, axis=0), payload=jnp.concatenate(cid, axis=0))

    e = jnp.exp(s_fin - s_fin[0:1, :])
    gates_ref[0] = e / jnp.sum(e, axis=0, keepdims=True)
    ids_ref[0] = ids.astype(I32)


def _topk(qp, sk_bf, first_token, n_tokens):
    tt = TILE_TOPK
    first = first_token // tt
    out_blk = pl.BlockSpec((1, PEER_TOPK, tt), lambda i, h: (h, 0, i))
    return pl.pallas_call(
        _topk_body,
        grid=(n_tokens // tt, PEER_HEADS),
        in_specs=[
            pl.BlockSpec((2, tt, PEER_HALF), lambda i, h: (h, first + i, 0)),
            pl.BlockSpec((2, N_KEYS, PEER_HALF), lambda i, h: (h, 0, 0)),
        ],
        out_specs=[out_blk, out_blk],
        out_shape=[
            jax.ShapeDtypeStruct((PEER_HEADS, PEER_TOPK, n_tokens), I32),
            jax.ShapeDtypeStruct((PEER_HEADS, PEER_TOPK, n_tokens), F32),
        ],
        compiler_params=pltpu.CompilerParams(
            dimension_semantics=("parallel", "parallel"), vmem_limit_bytes=VMEM_LIMIT),
        name="peer_topk",
    )(qp, sk_bf)


def _route_all(u_tbl, v_tbl, ids, gates, h, h_first):
    s = ids.shape[0]
    tpw = s // SC_WORKERS
    assert tpw * SC_WORKERS == s and tpw >= 2 and ids.shape[1] == PEER_PICKS
    nq = PEER_PICKS // SC_QUARTER
    n_chunks = PACKED_WIDTH // SC_LANES
    mesh = plsc.VectorSubcoreMesh(core_axis_name="core", subcore_axis_name="subcore")

    @functools.partial(
        pl.kernel,
        out_type=jax.ShapeDtypeStruct((s, D_MODEL), F32),
        mesh=mesh,
        scratch_types=[
            pltpu.VMEM((tpw, PEER_PICKS), I32),
            pltpu.VMEM((tpw, PEER_PICKS), F32),
            pltpu.VMEM((2, SC_QUARTER, PACKED_WIDTH), U32),
            pltpu.VMEM((2, SC_QUARTER, PACKED_WIDTH), U32),
            pltpu.VMEM((2, D_MODEL), F32),
            pltpu.VMEM((2, D_MODEL), F32),
            pltpu.VMEM((PEER_PICKS,), F32),
            pltpu.VMEM((PEER_PICKS, SC_LANES), F32),
            pltpu.SemaphoreType.DMA((2,)),
            pltpu.SemaphoreType.DMA((2,)),
            pltpu.SemaphoreType.DMA((2,)),
            pltpu.SemaphoreType.DMA((2,)),
        ],
        compiler_params=pltpu.CompilerParams(needs_layout_passes=False),
    )
    def route_kernel(u_hbm, v_hbm, idx_hbm, g_hbm, h_hbm, y_hbm,
                     idx_v, gw_v, u_v, v_v, h_v, y_v, a_v, wb_v, u_sem, v_sem, h_sem, y_sem):
        worker = lax.axis_index("subcore") * SC_CORES + lax.axis_index("core")
        tok0 = worker * tpw
        pltpu.sync_copy(idx_hbm.at[pl.ds(tok0, tpw)], idx_v)
        pltpu.sync_copy(g_hbm.at[pl.ds(tok0, tpw)], gw_v)
        lane = lax.iota(I32, SC_LANES)

        def u_gather(t, q, b):
            idx = idx_v.at[t, pl.ds(q * SC_QUARTER, SC_QUARTER)]
            return pltpu.make_async_copy(u_hbm.at[idx], u_v.at[b], u_sem.at[b])

        def v_gather(t, q, b):
            idx = idx_v.at[t, pl.ds(q * SC_QUARTER, SC_QUARTER)]
            return pltpu.make_async_copy(v_hbm.at[idx], v_v.at[b], v_sem.at[b])

        def h_load(t, hb):
            return pltpu.make_async_copy(h_hbm.at[h_first + tok0 + t], h_v.at[hb], h_sem.at[hb])

        def y_store(t, yb):
            return pltpu.make_async_copy(y_v.at[yb], y_hbm.at[tok0 + t], y_sem.at[yb])

        def unpack(words):
            return (lax.bitcast_convert_type(words << 16, F32),
                    lax.bitcast_convert_type(words & jnp.uint32(0xFFFF0000), F32))

        h_load(0, 0).start()
        u_gather(0, 0, 0).start()
        v_gather(0, 0, 0).start()

        @pl.loop(0, tpw + 1)
        def _(t):
            hb = t % 2
            tv = t - 1
            yb = (t + 1) % 2
            do_u = t < tpw
            do_v = t >= 1

            @pl.when(do_u)
            def _():
                h_load(t, hb).wait()

                @pl.when(t + 1 < tpw)
                def _():
                    h_load(t + 1, 1 - hb).start()

            @pl.when(tv >= 2)
            def _():
                y_store(tv - 2, yb).wait()

            @pl.loop(0, nq)
            def _(q):
                b = q % 2
                last_q = q == nq - 1
                qn = jnp.where(last_q, 0, q + 1)

                @pl.when(do_u)
                def _():
                    u_gather(t, q, b).wait()
                    tn = jnp.where(last_q, t + 1, t)

                    @pl.when(tn < tpw)
                    def _():
                        u_gather(tn, qn, 1 - b).start()

                    for g in range(SC_QUARTER // SC_LANES):
                        def two_picks(j2, vec):
                            k0 = g * SC_LANES + j2 * 2
                            accs = [jnp.zeros((SC_LANES,), F32) for _ in range(2)]
                            for c in range(n_chunks):
                                h_lo = h_v[hb, pl.ds(SC_LANES * c, SC_LANES)]
                                h_hi = h_v[hb, pl.ds(PACKED_WIDTH + SC_LANES * c, SC_LANES)]
                                for p in range(2):
                                    lo, hi = unpack(u_v[b, k0 + p, pl.ds(SC_LANES * c, SC_LANES)])
                                    accs[p] = accs[p] + lo * h_lo + hi * h_hi
                            for p in range(2):
                                vec = jnp.where(lane == j2 * 2 + p, jnp.sum(accs[p]), vec)
                            return vec

                        vec = lax.fori_loop(0, SC_LANES // 2, two_picks, jnp.zeros((SC_LANES,), F32))
                        a_v[pl.ds(q * SC_QUARTER + g * SC_LANES, SC_LANES)] = vec

                @pl.when(do_v)
                def _():
                    v_gather(tv, q, b).wait()
                    tvn = jnp.where(last_q, tv + 1, tv)

                    @pl.when(tvn < tpw)
                    def _():
                        v_gather(tvn, qn, 1 - b).start()

                    first_q = q == 0
                    for fg in range(n_chunks // SC_ACC_CHUNKS):
                        lo_at = [pl.ds(SC_LANES * (fg * SC_ACC_CHUNKS + c), SC_LANES)
                                 for c in range(SC_ACC_CHUNKS)]
                        hi_at = [pl.ds(PACKED_WIDTH + SC_LANES * (fg * SC_ACC_CHUNKS + c), SC_LANES)
                                 for c in range(SC_ACC_CHUNKS)]
                        zero = jnp.zeros((SC_LANES,), F32)
                        init = tuple([jnp.where(first_q, zero, y_v[yb, sl]) for sl in lo_at]
                                     + [jnp.where(first_q, zero, y_v[yb, sl]) for sl in hi_at])

                        def two_rows(k2, accs, lo_at=lo_at):
                            accs = list(accs)
                            for p in range(2):
                                kk = k2 * 2 + p
                                wk = wb_v[q * SC_QUARTER + kk]
                                for c in range(SC_ACC_CHUNKS):
                                    lo, hi = unpack(v_v[b, kk, lo_at[c]])
                                    accs[c] = accs[c] + wk * lo
                                    accs[SC_ACC_CHUNKS + c] = accs[SC_ACC_CHUNKS + c] + wk * hi
                            return tuple(accs)

                        accs = lax.fori_loop(0, SC_QUARTER // 2, two_rows, init)
                        for c in range(SC_ACC_CHUNKS):
                            y_v[yb, lo_at[c]] = accs[c]
                            y_v[yb, hi_at[c]] = accs[SC_ACC_CHUNKS + c]

            @pl.when(do_v)
            def _():
                y_store(tv, yb).start()

            @pl.when(do_u)
            def _():
                for g in range(PEER_PICKS // SC_LANES):
                    sl = pl.ds(g * SC_LANES, SC_LANES)
                    a = a_v[sl]
                    z = GELU_TANH_SCALE * (a + GELU_TANH_CUBIC * a * a * a)
                    tanh_z = 1.0 - 2.0 / (jnp.exp(2.0 * z) + 1.0)
                    w = gw_v[t, sl] * (0.5 * a * (1.0 + tanh_z))
                    for j in range(SC_LANES):
                        lane_j = jnp.full((SC_LANES,), j, I32)
                        wb_v[g * SC_LANES + j] = w.at[lane_j].get(mode="promise_in_bounds")

        y_store(tpw - 1, (tpw - 1) % 2).wait()
        y_store(tpw - 2, (tpw - 2) % 2).wait()

    return route_kernel(u_tbl, v_tbl, ids, gates, h)


def _finish_body(x1_ref, y_ref, gf_ref, o_ref):
    o_ref[...] = _rms(x1_ref[...] + y_ref[...], gf_ref[...])


def _finish(y, x1, gf, first_token):
    n = y.shape[0]
    tt = TILE_MERGE
    first = first_token // tt
    resid = pl.BlockSpec((tt, D_MODEL), lambda i: (first + i, 0))
    return pl.pallas_call(
        _finish_body,
        grid=(n // tt,),
        in_specs=[resid, pl.BlockSpec((tt, D_MODEL), lambda i: (i, 0)), _const_spec((1, D_MODEL))],
        out_specs=resid,
        out_shape=jax.ShapeDtypeStruct(x1.shape, F32),
        input_output_aliases={0: 0},
        compiler_params=pltpu.CompilerParams(
            dimension_semantics=("parallel",), vmem_limit_bytes=VMEM_LIMIT),
        name="peer_finish",
    )(x1, y, gf)


def _pack_body(t_ref, o_ref):
    bits = lax.bitcast_convert_type(t_ref[...].astype(BF16).astype(F32), U32)
    o_ref[...] = (bits[:, :PACKED_WIDTH] >> 16) | (bits[:, PACKED_WIDTH:] & jnp.uint32(0xFFFF0000))


def _pack_rows(tbl):
    rows = tbl.shape[0]
    tile = TILE_PACK
    return pl.pallas_call(
        _pack_body,
        grid=(rows // tile,),
        in_specs=[pl.BlockSpec((tile, D_MODEL), lambda i: (i, 0))],
        out_specs=pl.BlockSpec((tile, PACKED_WIDTH), lambda i: (i, 0)),
        out_shape=jax.ShapeDtypeStruct((rows, PACKED_WIDTH), U32),
        compiler_params=pltpu.CompilerParams(
            dimension_semantics=("parallel",), vmem_limit_bytes=VMEM_LIMIT),
        name="pack_rows",
    )(tbl)


def _route_pieces(b, batch, seq):
    half, small = seq // 2, seq // ROUTE_TAPER
    taper = small % max(TILE_TOPK, SC_WORKERS * 2) == 0 and small < half
    head = [small, half - small] if (taper and b == batch - 1) else [half]
    tail = [half - small, small] if (taper and b == 0) else [half]
    return head + tail


def kernel(x, mem, positions, norm1_g, mem_norm_g, w_in, w_pool, pool_scale, attn_sinks, w_mem_kv, w_branch_pool, w_branch_swa, w_branch_mem, w_out, norm2_g, peer_w_q, peer_sub_keys, peer_u, peer_v, final_norm_g):
    batch, seq, d = x.shape
    assert norm1_g.shape[0] == 1, "single-layer trunk"
    half = SWA_HEAD_DIM // 2
    inv = ROPE_THETA ** (-jnp.arange(half, dtype=F32) / half)
    invf = jnp.tile(inv, 128 // half).reshape(1, 128)
    row = lambda v: v.reshape(1, -1)

    w_in_bf = w_in[0].astype(BF16)
    merge_weights = (
        w_pool[0].astype(BF16), row(pool_scale[0]), w_branch_pool[0].astype(BF16),
        w_branch_swa[0].astype(BF16), w_branch_mem[0].astype(BF16), w_out[0].astype(BF16),
        row(norm2_g[0]), peer_w_q[0].astype(BF16))
    sk = peer_sub_keys[0].reshape(2 * PEER_HEADS, N_KEYS, PEER_HALF).astype(BF16)
    u_pack = _pack_rows(peer_u[0])
    v_pack = _pack_rows(peer_v[0])
    gf = row(final_norm_g)
    km, vm = _memkv(mem, row(mem_norm_g[0]), w_mem_kv[0].astype(BF16))

    outs = []
    for b in range(batch):
        xb = x[b]
        up, q, k, v, qm, gl = _inproj(xb, row(norm1_g[0]), w_in_bf)
        ys = _swa(q, k, v, positions[b].reshape(seq, 1), attn_sinks[0], invf, 1, seq)
        out, h2, qp = _merge(xb, up, ys, qm, gl, km[b:b + 1], vm[b:b + 1], *merge_weights, seq)
        first = 0
        for piece in _route_pieces(b, batch, seq):
            ids_t, gates_t = _topk(qp, sk, first, piece)
            ids = ids_t.reshape(PEER_PICKS, piece).T
            gates = gates_t.reshape(PEER_PICKS, piece).T
            y = _route_all(u_pack, v_pack, ids, gates, h2, first)
            out = _finish(y, out, gf, first)
            first += piece
        outs.append(out)
    return jnp.stack(outs, axis=0)
```
